```python
import jax, jax.numpy as jnp
from jax import lax
import numpy as np

D_MODEL = 1024
BATCH = 16
SEQ = 2048
DEPTH = 1
DEC_BATCH = 32
DEC_SEQ = 4
PAST_LEN = 16384
PAGE_SIZE = 128

MIX_WIDTH = D_MODEL
A_WIDTH = MIX_WIDTH // 2
B_WIDTH = MIX_WIDTH - A_WIDTH
HGRN_HEAD_DIM = 128
H_A = A_WIDTH // HGRN_HEAD_DIM
HGRN_CHUNK = 64
HD_B = 64
H_B = B_WIDTH // HD_B
ROT_DIM = HD_B // 4
ROPE_THETA = 500000.0
DILATIONS = ((128, 1), (512, 4), (2048, 16))
WIN_MAX = 2048
ATTN_BLOCK = 128
D_FF = 2816
CONV_W = 3
PLE_DIM = 256
EPS = 1e-6
NEG_INF = -1e30
IN_SPLITS = (A_WIDTH, A_WIDTH, A_WIDTH, A_WIDTH, B_WIDTH, B_WIDTH, B_WIDTH)
D_IN = A_WIDTH * 4 + B_WIDTH * 3

kernel_name = "hymba_hgrn2_dilated_swa_convffn_step"


def rmsnorm(x, g):
    x32 = x.astype(jnp.float32)
    y = x32 * lax.rsqrt(jnp.mean(x32 * x32, axis=-1, keepdims=True) + EPS)
    return (y * g.astype(jnp.float32)).astype(x.dtype)


def partial_rotary(x, pos):
    half = ROT_DIM // 2
    inv_freq = jnp.power(ROPE_THETA, -jnp.arange(half, dtype=jnp.float32) * (2.0 / ROT_DIM))
    ang = pos[:, None] * inv_freq[None, :]
    cos = jnp.cos(ang)[None, :, None, :]
    sin = jnp.sin(ang)[None, :, None, :]
    x1 = x[..., :half]
    x2 = x[..., half:ROT_DIM]
    return jnp.concatenate([x1 * cos - x2 * sin, x2 * cos + x1 * sin, x[..., ROT_DIM:]], axis=-1)


def gla_chunked(q, k, v, logf, s0, chunk):
    bsz, t_len, nh, dk = q.shape
    dv = v.shape[-1]
    nc = t_len // chunk

    def to_chunks(t):
        return t.reshape(bsz, nc, chunk, nh, t.shape[-1]).transpose(1, 0, 3, 2, 4)

    causal = jnp.tril(jnp.ones((chunk, chunk), dtype=bool))

    def step(s, xs):
        qc, kc, vc, gc = xs
        b = jnp.cumsum(gc, axis=2)
        o_inter = jnp.einsum('bhtk,bhkv->bhtv', qc * jnp.exp(b), s)
        diff = b[:, :, :, None, :] - b[:, :, None, :, :]
        decay = jnp.exp(jnp.where(causal[:, :, None], diff, -jnp.inf))
        scores = jnp.einsum('bhtk,bhsk,bhtsk->bhts', qc, kc, decay)
        o = o_inter + jnp.einsum('bhts,bhsv->bhtv', scores, vc)
        b_last = b[:, :, -1]
        s = jnp.exp(b_last)[..., None] * s + jnp.einsum(
            'bhsk,bhsv->bhkv', kc * jnp.exp(b_last[:, :, None] - b), vc)
        return s, o

    s_fin, o = lax.scan(step, s0, (to_chunks(q), to_chunks(k), to_chunks(v), to_chunks(logf)))
    o = o.transpose(1, 0, 3, 2, 4).reshape(bsz, t_len, nh, dv)
    return s_fin, o


def combine_dilations(parts):
    ms = jnp.stack([p[0] for p in parts])
    ls = jnp.stack([p[1] for p in parts])
    accs = jnp.stack([p[2] for p in parts])
    w = jnp.exp(ms - ms.max(axis=0, keepdims=True))
    return (w[..., None] * accs).sum(0) / (w * ls).sum(0)[..., None]


def dilated_attn_prompt(q, k, v):
    bsz, s_len, nh, hd = q.shape
    scale = hd ** -0.5
    qi = jnp.arange(ATTN_BLOCK)[:, None]
    kj = jnp.arange(2 * ATTN_BLOCK)[None, :]
    dist = qi - kj + ATTN_BLOCK
    parts = []
    for window, dil in DILATIONS:
        n_back = window // dil
        sub_len = s_len // dil
        nb = -(-sub_len // ATTN_BLOCK)
        sub_pad = nb * ATTN_BLOCK

        def to_sub(t):
            return t.reshape(bsz, sub_len, dil, nh, hd).transpose(0, 2, 1, 3, 4)

        qs = jnp.pad(to_sub(q * scale), ((0, 0), (0, 0), (0, sub_pad - sub_len), (0, 0), (0, 0)))
        qs = qs.reshape(bsz, dil, nb, ATTN_BLOCK, nh, hd)

        def key_blocks(t):
            t = jnp.pad(to_sub(t), ((0, 0), (0, 0), (ATTN_BLOCK, sub_pad - sub_len), (0, 0), (0, 0)))
            t = t.reshape(bsz, dil, nb + 1, ATTN_BLOCK, nh, hd)
            return jnp.concatenate([t[:, :, :-1], t[:, :, 1:]], axis=3)

        kb = key_blocks(k)
        vb = key_blocks(v)
        blk = jnp.arange(nb)[:, None, None]
        valid = (dist >= 0) & (dist <= n_back) & (blk * ATTN_BLOCK - ATTN_BLOCK + kj >= 0)
        s = jnp.einsum('brnqhc,brnkhc->brnhqk', qs, kb)
        s = jnp.where(valid[None, None, :, None], s, NEG_INF)
        m = s.max(axis=-1)
        p = jnp.exp(s - m[..., None])
        l = p.sum(axis=-1)
        acc = jnp.einsum('brnhqk,brnkhc->brnqhc', p, vb)

        def from_sub(t):
            t = t.reshape((bsz, dil, sub_pad) + t.shape[4:])[:, :, :sub_len]
            t = jnp.moveaxis(t, 1, 2)
            return t.reshape((bsz, s_len) + t.shape[3:])

        parts.append((from_sub(jnp.swapaxes(m, 3, 4)), from_sub(jnp.swapaxes(l, 3, 4)), from_sub(acc)))
    return combine_dilations(parts)


def dilated_attn_sample(q, k_all, v_all):
    t_len = q.shape[1]
    wb = k_all.shape[1] - t_len
    scale = q.shape[-1] ** -0.5
    parts = []
    for window, dil in DILATIONS:
        n_back = window // dil
        idx = wb + jnp.arange(t_len)[:, None] - dil * jnp.arange(n_back + 1)[None, :]
        valid = idx >= 0
        idxc = jnp.maximum(idx, 0)
        kg = k_all[:, idxc]
        vg = v_all[:, idxc]
        s = jnp.einsum('bthc,btjhc->bthj', q * scale, kg)
        s = jnp.where(valid[None, :, None, :], s, NEG_INF)
        m = s.max(axis=-1)
        p = jnp.exp(s - m[..., None])
        l = p.sum(axis=-1)
        acc = jnp.einsum('bthj,btjhc->bthc', p, vg)
        parts.append((m, l, acc))
    return combine_dilations(parts)


def layer_step(x, p_emb, pos0, hgrn_s0, win_k, win_v, conv_buf, lb,
               norm_attn_g, w_in, hgrn_onorm_g, w_o, norm_ffn_g, w_gate, w_up,
               conv_w, conv_b, w_down, norm_ple_g, w_ple_gate, w_ple_proj, is_prompt):
    bsz, t_len, _ = x.shape
    dt = x.dtype
    f32 = jnp.float32
    a = rmsnorm(x, norm_attn_g)
    proj = a @ w_in
    pts = [sum(IN_SPLITS[:j]) for j in range(1, len(IN_SPLITS))]
    qa, fa, ia, ga, qb, kb, vb = jnp.split(proj, pts, axis=-1)

    lb_h = lb.reshape(H_A, HGRN_HEAD_DIM)
    q_a = jax.nn.silu(qa.astype(f32).reshape(bsz, t_len, H_A, HGRN_HEAD_DIM)) * HGRN_HEAD_DIM ** -0.5
    forget = lb_h + (1.0 - lb_h) * jax.nn.sigmoid(fa.astype(f32).reshape(bsz, t_len, H_A, HGRN_HEAD_DIM))
    k_a = 1.0 - forget
    v_a = ia.astype(f32).reshape(bsz, t_len, H_A, HGRN_HEAD_DIM)
    chunk = min(HGRN_CHUNK, t_len) if is_prompt else t_len
    s_new, o_a = gla_chunked(q_a, k_a, v_a, jnp.log(forget), hgrn_s0, chunk)
    o_a = rmsnorm(o_a, hgrn_onorm_g.reshape(H_A, HGRN_HEAD_DIM)).reshape(bsz, t_len, A_WIDTH)
    o_a = o_a * jax.nn.silu(ga.astype(f32))

    pos = pos0 + jnp.arange(t_len, dtype=f32)
    q_b = partial_rotary(qb.astype(f32).reshape(bsz, t_len, H_B, HD_B), pos)
    k_b = partial_rotary(kb.astype(f32).reshape(bsz, t_len, H_B, HD_B), pos)
    v_b = vb.astype(f32).reshape(bsz, t_len, H_B, HD_B)
    if is_prompt:
        o_b = dilated_attn_prompt(q_b, k_b, v_b)
        keep = min(WIN_MAX, t_len)
        new_k = k_b[:, t_len - keep:]
        new_v = v_b[:, t_len - keep:]
    else:
        k_all = jnp.concatenate([win_k.astype(f32), k_b], axis=1)
        v_all = jnp.concatenate([win_v.astype(f32), v_b], axis=1)
        o_b = dilated_attn_sample(q_b, k_all, v_all)
        keep = win_k.shape[1]
        new_k = k_all[:, -keep:]
        new_v = v_all[:, -keep:]

    mix = jnp.concatenate([o_a.astype(dt), o_b.reshape(bsz, t_len, B_WIDTH).astype(dt)], axis=-1)
    h = x + mix @ w_o

    hn = rmsnorm(h, norm_ffn_g)
    u = hn @ w_gate
    full = jnp.concatenate([conv_buf.astype(dt), u], axis=1)
    c = conv_b
    for j in range(CONV_W):
        c = c + conv_w[j] * full[:, j:j + t_len]
    h = h + (jax.nn.silu(c) * (hn @ w_up)) @ w_down
    new_conv = full[:, t_len:]

    gate = jax.nn.sigmoid(rmsnorm(h, norm_ple_g) @ w_ple_gate)
    h = h + gate * (p_emb.astype(dt) @ w_ple_proj)
    return h, s_new, new_k, new_v, new_conv


def setup_inputs(seed: int = 0) -> dict:
    key = jax.random.key(seed)
    ks = jax.random.split(key, 24)
    f32 = jnp.float32
    wb = min(WIN_MAX, PAST_LEN)

    def nrm(k, shape, s=1.0):
        return jax.random.normal(k, shape, f32) * s

    def gain(k, shape):
        return 1.0 + 0.02 * jax.random.normal(k, shape, f32)

    return {
        "x_prompt": nrm(ks[0], (BATCH, SEQ, D_MODEL)),
        "x_sample": nrm(ks[1], (DEC_BATCH, DEC_SEQ, D_MODEL)),
        "state_hgrn": nrm(ks[2], (DEPTH, DEC_BATCH, H_A, HGRN_HEAD_DIM, HGRN_HEAD_DIM), 0.5),
        "cache_win_k": nrm(ks[3], (DEPTH, DEC_BATCH, wb, H_B, HD_B)),
        "cache_win_v": nrm(ks[4], (DEPTH, DEC_BATCH, wb, H_B, HD_B)),
        "state_ffn_conv": nrm(ks[5], (DEPTH, DEC_BATCH, CONV_W - 1, D_FF)),
        "p_prompt": nrm(ks[6], (DEPTH, BATCH, SEQ, PLE_DIM)),
        "p_sample": nrm(ks[7], (DEPTH, DEC_BATCH, DEC_SEQ, PLE_DIM)),
        "norm_attn_g": gain(ks[8], (DEPTH, D_MODEL)),
        "w_in": nrm(ks[9], (DEPTH, D_MODEL, D_IN), D_MODEL ** -0.5),
        "hgrn_lb_logits": nrm(ks[10], (DEPTH + 1, A_WIDTH), 0.1),
        "hgrn_onorm_g": gain(ks[11], (DEPTH, A_WIDTH)),
        "w_o": nrm(ks[12], (DEPTH, MIX_WIDTH, D_MODEL), MIX_WIDTH ** -0.5),
        "norm_ffn_g": gain(ks[13], (DEPTH, D_MODEL)),
        "w_gate": nrm(ks[14], (DEPTH, D_MODEL, D_FF), D_MODEL ** -0.5),
        "w_up": nrm(ks[15], (DEPTH, D_MODEL, D_FF), D_MODEL ** -0.5),
        "conv_w": nrm(ks[16], (DEPTH, CONV_W, D_FF), CONV_W ** -0.5),
        "conv_b": nrm(ks[17], (DEPTH, D_FF), 0.01),
        "w_down": nrm(ks[18], (DEPTH, D_FF, D_MODEL), D_FF ** -0.5),
        "norm_ple_g": gain(ks[19], (DEPTH, D_MODEL)),
        "w_ple_gate": nrm(ks[20], (DEPTH, D_MODEL, D_MODEL), D_MODEL ** -0.5),
        "w_ple_proj": nrm(ks[21], (DEPTH, PLE_DIM, D_MODEL), PLE_DIM ** -0.5),
        "norm_final_g": gain(ks[22], (D_MODEL,)),
    }


def reference(x_prompt, x_sample, state_hgrn, cache_win_k, cache_win_v, state_ffn_conv,
              p_prompt, p_sample, norm_attn_g, w_in, hgrn_lb_logits, hgrn_onorm_g, w_o,
              norm_ffn_g, w_gate, w_up, conv_w, conv_b, w_down, norm_ple_g, w_ple_gate,
              w_ple_proj, norm_final_g):
    f32 = jnp.float32
    lb_all = jnp.cumsum(jax.nn.softmax(hgrn_lb_logits.astype(f32), axis=0), axis=0)
    hp = x_prompt
    hs = x_sample
    p_s, p_k, p_v, p_c = [], [], [], []
    s_s, s_k, s_v, s_c = [], [], [], []
    for i in range(DEPTH):
        weights = (norm_attn_g[i], w_in[i], hgrn_onorm_g[i], w_o[i], norm_ffn_g[i], w_gate[i],
                   w_up[i], conv_w[i], conv_b[i], w_down[i], norm_ple_g[i], w_ple_gate[i],
                   w_ple_proj[i])
        s0_p = jnp.zeros((hp.shape[0], H_A, HGRN_HEAD_DIM, HGRN_HEAD_DIM), f32)
        conv0_p = jnp.zeros((hp.shape[0], CONV_W - 1, D_FF), hp.dtype)
        hp, a1, a2, a3, a4 = layer_step(hp, p_prompt[i], 0.0, s0_p, None, None, conv0_p,
                                        lb_all[i], *weights, True)
        hs, b1, b2, b3, b4 = layer_step(hs, p_sample[i], float(PAST_LEN), state_hgrn[i].astype(f32),
                                        cache_win_k[i], cache_win_v[i], state_ffn_conv[i],
                                        lb_all[i], *weights, False)
        p_s.append(a1); p_k.append(a2); p_v.append(a3); p_c.append(a4)
        s_s.append(b1); s_k.append(b2); s_v.append(b3); s_c.append(b4)
    y_prompt = rmsnorm(hp, norm_final_g)
    y_sample = rmsnorm(hs, norm_final_g)
    prompt_state_hgrn = jnp.stack(p_s)
    prompt_cache_win_k = jnp.stack(p_k)
    prompt_cache_win_v = jnp.stack(p_v)
    prompt_state_ffn_conv = jnp.stack(p_c)
    sample_state_hgrn = jnp.stack(s_s)
    sample_cache_win_k = jnp.stack(s_k)
    sample_cache_win_v = jnp.stack(s_v)
    sample_state_ffn_conv = jnp.stack(s_c)
    return (y_prompt, y_sample, prompt_state_hgrn, prompt_cache_win_k, prompt_cache_win_v,
            prompt_state_ffn_conv, sample_state_hgrn, sample_cache_win_k, sample_cache_win_v,
            sample_state_ffn_conv)
```

```python
import functools

import numpy as np
import jax
import jax.numpy as jnp
from jax import lax
from jax.experimental import pallas as pl
from jax.experimental.pallas import tpu as pltpu

F32 = jnp.float32
BF16 = jnp.bfloat16

HGRN_HEAD_DIM = 128
HD_B = 64
ROT_DIM = HD_B // 4
ROPE_THETA = 500000.0
DILATIONS = ((128, 1), (512, 4), (2048, 16))
WIN_MAX = 2048
PAST_LEN = 16384
HGRN_CHUNK = 64
CONV_W = 3
EPS = 1e-6
NEG_INF = -1e30

SUBLANES = 8
LANES = 128
MXU_WIDTH = 256
VMEM_LIMIT_BYTES = 56 * 1024 * 1024

NT_DIMS = (((1,), (1,)), ((), ()))
TN_DIMS = (((0,), (0,)), ((), ()))


def _rmsnorm(x, g):
    return x * lax.rsqrt(jnp.mean(x * x, axis=-1, keepdims=True) + EPS) * g


def _sigmoid(x):
    return 1.0 / (1.0 + jnp.exp(-x))


def _silu(x):
    return x * _sigmoid(x)


def _const_spec(shape):
    nd = len(shape)
    return pl.BlockSpec(shape, lambda *_: (0,) * nd, pipeline_mode=pl.Buffered(1))


def _params(semantics):
    return pltpu.CompilerParams(dimension_semantics=semantics, vmem_limit_bytes=VMEM_LIMIT_BYTES)


def _proj_kernel(x_ref, g_ref, w_ref, lbl_ref, cos_ref, sin_ref,
                 a_ref, qb_ref, kb_ref, vb_ref, *, layer, a_width, b_width):
    x = x_ref[0]
    a = _rmsnorm(x, g_ref[...]).astype(BF16)

    def pj(j0, width):
        return jnp.dot(a, w_ref[:, j0:j0 + width], preferred_element_type=F32)

    lbl = lbl_ref[...]
    e = jnp.exp(lbl - jnp.max(lbl, axis=0, keepdims=True))
    lb = jnp.sum(e[:layer + 1], axis=0, keepdims=True) / jnp.sum(e, axis=0, keepdims=True)

    aw = a_width
    a_ref[0, :, 0:aw] = _silu(pj(0, aw)) * (HGRN_HEAD_DIM ** -0.5)
    a_ref[0, :, aw:2 * aw] = lb + (1.0 - lb) * _sigmoid(pj(aw, aw))
    a_ref[0, :, 2 * aw:3 * aw] = pj(2 * aw, aw)
    a_ref[0, :, 3 * aw:4 * aw] = _silu(pj(3 * aw, aw))

    lane = lax.broadcasted_iota(jnp.int32, (1, LANES), 1) % HD_B
    first_half = lane < (ROT_DIM // 2)
    cos = cos_ref[...]
    sin = sin_ref[...]

    def rotary(t):
        up = pltpu.roll(t, LANES - ROT_DIM // 2, 1)
        dn = pltpu.roll(t, ROT_DIM // 2, 1)
        return t * cos + jnp.where(first_half, up, dn) * sin

    bw = b_width
    q = pj(4 * aw, bw)
    k = pj(4 * aw + bw, bw)
    for c in range(bw // LANES):
        sl = slice(c * LANES, (c + 1) * LANES)
        qb_ref[0, :, sl] = (rotary(q[:, sl]) * (HD_B ** -0.5)).astype(qb_ref.dtype)
        kb_ref[0, :, sl] = rotary(k[:, sl])
    vb_ref[0] = pj(4 * aw + 2 * bw, bw)


def _proj(x, g, w_in, lb_logits, cos_t, sin_t, *, layer, ts):
    bsz, t_len, d = x.shape
    a_width = lb_logits.shape[1]
    b_width = (w_in.shape[1] - 4 * a_width) // 3
    nt = t_len // ts
    tok = lambda j, b: (b, j, 0)
    return pl.pallas_call(
        functools.partial(_proj_kernel, layer=layer, a_width=a_width, b_width=b_width),
        grid=(nt, bsz),
        in_specs=[
            pl.BlockSpec((1, ts, d), tok),
            _const_spec((1, d)),
            _const_spec(w_in.shape),
            _const_spec(lb_logits.shape),
            pl.BlockSpec((ts, LANES), lambda j, b: (j, 0)),
            pl.BlockSpec((ts, LANES), lambda j, b: (j, 0)),
        ],
        out_specs=[
            pl.BlockSpec((1, ts, 4 * a_width), tok),
            pl.BlockSpec((1, ts, b_width), tok),
            pl.BlockSpec((1, ts, b_width), tok),
            pl.BlockSpec((1, ts, b_width), tok),
        ],
        out_shape=[
            jax.ShapeDtypeStruct((bsz, t_len, 4 * a_width), F32),
            jax.ShapeDtypeStruct((bsz, t_len, b_width), BF16),
            jax.ShapeDtypeStruct((bsz, t_len, b_width), F32),
            jax.ShapeDtypeStruct((bsz, t_len, b_width), F32),
        ],
        compiler_params=_params(("parallel", "parallel")),
        name="proj",
    )(x, g.reshape(1, d), w_in, lb_logits, cos_t, sin_t)


def _rotary_tables(pos):
    half = ROT_DIM // 2
    inv_freq = jnp.power(ROPE_THETA, -jnp.arange(half, dtype=F32) * (2.0 / ROT_DIM))
    ang = pos[:, None] * inv_freq[None, :]
    cos, sin = jnp.cos(ang), jnp.sin(ang)
    t_len = pos.shape[0]
    ones = jnp.ones((t_len, HD_B - ROT_DIM), F32)
    cos_h = jnp.concatenate([cos, cos, ones], axis=1)
    sin_h = jnp.concatenate([-sin, sin, 0.0 * ones], axis=1)
    reps = LANES // HD_B
    return jnp.tile(cos_h, (1, reps)), jnp.tile(sin_h, (1, reps))


def _boundary_rows(b_ref, chunk, half):
    sub = lax.broadcasted_iota(jnp.int32, (SUBLANES, 1), 0)
    pieces = []
    for grp in range(chunk // SUBLANES):
        base = grp * SUBLANES
        if 2 * half >= SUBLANES:
            r = (base // (2 * half)) * (2 * half) + half - 1
            piece = jnp.broadcast_to(b_ref[r:r + 1, :], (SUBLANES, LANES))
        else:
            piece = None
            for blk in range(SUBLANES // (2 * half)):
                r = base + blk * 2 * half + half - 1
                row = jnp.broadcast_to(b_ref[r:r + 1, :], (SUBLANES, LANES))
                piece = row if piece is None else jnp.where(sub >= blk * 2 * half, row, piece)
        pieces.append(piece)
    return pieces[0] if len(pieces) == 1 else jnp.concatenate(pieces, axis=0)


def _gla_kernel(q_ref, f_ref, v_ref, gs_ref, s0_ref, on_ref, o_ref, s_ref, st_ref, b_ref, *, chunk):
    t_len = q_ref.shape[1]
    st_ref[...] = s0_ref[0, 0].T

    row = lax.broadcasted_iota(jnp.int32, (chunk, chunk), 0)
    col = lax.broadcasted_iota(jnp.int32, (chunk, chunk), 1)
    differ = row ^ col
    lower = row > col
    rows1 = lax.broadcasted_iota(jnp.int32, (chunk, 1), 0)
    halves = []
    h = chunk // 2
    while h >= 1:
        halves.append(h)
        h //= 2
    onorm_g = on_ref[...]

    def body(c, carry):
        r0 = pl.multiple_of(c * chunk, chunk)
        q = q_ref[0, pl.ds(r0, chunk), :]
        fg = f_ref[0, pl.ds(r0, chunk), :]
        v = v_ref[0, pl.ds(r0, chunk), :]
        g = jnp.log(fg)
        k = 1.0 - fg
        k16 = k.astype(BF16)
        v16 = v.astype(BF16)

        b = g
        d = 1
        while d < chunk:
            b = b + jnp.where(rows1 >= d, pltpu.roll(b, d, 0), 0.0)
            d *= 2
        b_ref[...] = b

        a_mat = jnp.where(differ == 0,
                          lax.dot_general(q.astype(BF16), k16, NT_DIMS, preferred_element_type=F32),
                          0.0)
        for half in halves:
            mid = _boundary_rows(b_ref, chunk, half)
            e = jnp.exp(-jnp.abs(b - mid))
            lvl = lax.dot_general((q * e).astype(BF16), (k * e).astype(BF16), NT_DIMS,
                                  preferred_element_type=F32)
            a_mat = a_mat + jnp.where(lower & (differ >= half) & (differ < 2 * half), lvl, 0.0)

        st = st_ref[...]
        o = lax.dot_general((q * jnp.exp(b)).astype(BF16), st.astype(BF16), NT_DIMS,
                            preferred_element_type=F32)
        o = o + jnp.dot(a_mat.astype(BF16), v16, preferred_element_type=F32)

        b_last = b_ref[chunk - 1:chunk, :]
        kd = (k * jnp.exp(b_last - b)).astype(BF16)
        st_ref[...] = st * jnp.exp(b_last) + lax.dot_general(v16, kd, TN_DIMS,
                                                             preferred_element_type=F32)

        o = _rmsnorm(o, onorm_g) * gs_ref[0, pl.ds(r0, chunk), :]
        o_ref[0, pl.ds(r0, chunk), :] = o.astype(o_ref.dtype)
        return carry

    lax.fori_loop(0, t_len // chunk, body, 0)
    s_ref[0, 0] = st_ref[...].T


def _gla(a_part, s0, onorm_g, *, chunk):
    bsz, t_len, a4 = a_part.shape
    nh = s0.shape[1]
    hd = HGRN_HEAD_DIM
    col = lambda part: pl.BlockSpec((1, t_len, hd), lambda b, h, part=part: (b, 0, part * nh + h))
    return pl.pallas_call(
        functools.partial(_gla_kernel, chunk=chunk),
        grid=(bsz, nh),
        in_specs=[col(0), col(1), col(2), col(3),
                  pl.BlockSpec((1, 1, hd, hd), lambda b, h: (b, h, 0, 0)),
                  pl.BlockSpec((1, hd), lambda b, h: (0, h))],
        out_specs=[pl.BlockSpec((1, t_len, hd), lambda b, h: (b, 0, h)),
                   pl.BlockSpec((1, 1, hd, hd), lambda b, h: (b, h, 0, 0))],
        out_shape=[jax.ShapeDtypeStruct((bsz, t_len, a4 // 4), BF16),
                   jax.ShapeDtypeStruct(s0.shape, F32)],
        scratch_shapes=[pltpu.VMEM((hd, hd), F32), pltpu.VMEM((chunk, hd), F32)],
        compiler_params=_params(("parallel", "parallel")),
        name="gla",
    )(a_part, a_part, a_part, a_part, s0, onorm_g.reshape(1, -1))


def _log_multiplicity(dist):
    count = np.zeros(dist.shape, np.float64)
    for window, dil in DILATIONS:
        count += (dist >= 0) & (dist <= window) & (dist % dil == 0)
    return np.where(count > 0, np.log(np.maximum(count, 1.0)), NEG_INF).astype(np.float32)


ATTN_TILE = 256


def _attn_kernel(q_ref, k_ref, v_ref, bias_ref, o_ref, k16_ref, v16_ref):
    s_len = q_ref.shape[1]
    n_tiles = s_len // ATTN_TILE
    k16_ref[...] = k_ref[0].astype(BF16)
    v16_ref[...] = v_ref[0].astype(BF16)
    lane = lax.broadcasted_iota(jnp.int32, (1, LANES), 1)
    heads = [lane < HD_B, lane >= HD_B]
    for i in range(n_tiles):
        rows = slice(i * ATTN_TILE, (i + 1) * ATTN_TILE)
        n_keys = (i + 1) * ATTN_TILE
        q = q_ref[0, rows, :]
        outs = []
        for head in heads:
            s = lax.dot_general(jnp.where(head, q, jnp.zeros_like(q)), k16_ref[0:n_keys, :], NT_DIMS,
                                preferred_element_type=F32)
            s = s + bias_ref[:, s_len - n_keys:s_len]
            p = jnp.exp(s - jnp.max(s, axis=-1, keepdims=True))
            den = jnp.sum(p, axis=-1, keepdims=True)
            outs.append(jnp.dot(p.astype(BF16), v16_ref[0:n_keys, :], preferred_element_type=F32) / den)
        o_ref[0, rows, :] = jnp.where(heads[0], outs[0], outs[1]).astype(o_ref.dtype)


def _attn_prompt(q, k, v):
    bsz, s_len, bw = q.shape
    assert s_len % ATTN_TILE == 0 and bw % LANES == 0
    r = np.arange(ATTN_TILE)[:, None]
    c = np.arange(s_len)[None, :]
    bias = jnp.asarray(_log_multiplicity(r - (c - (s_len - ATTN_TILE))))
    blk = pl.BlockSpec((1, s_len, LANES), lambda b, h: (b, 0, h))
    return pl.pallas_call(
        _attn_kernel,
        grid=(bsz, bw // LANES),
        in_specs=[blk, blk, blk, _const_spec(bias.shape)],
        out_specs=blk,
        out_shape=jax.ShapeDtypeStruct((bsz, s_len, bw), BF16),
        scratch_shapes=[pltpu.VMEM((s_len, LANES), BF16), pltpu.VMEM((s_len, LANES), BF16)],
        compiler_params=_params(("parallel", "parallel")),
        name="attn",
    )(q, k, v, bias)


SHIFT_ROWS = 256


def _sattn_kernel(q_ref, kn_ref, vn_ref, ck_ref, cv_ref, bc_ref, bn_ref,
                  o_ref, ok_ref, ov_ref, *, t_len):
    wb = ck_ref.shape[1]
    bw = ck_ref.shape[2]
    nh = bw // HD_B
    assert nh == SUBLANES

    for src, new, dst in ((ck_ref, kn_ref, ok_ref), (cv_ref, vn_ref, ov_ref)):
        r0 = 0
        while r0 < wb - t_len:
            n = min(SHIFT_ROWS, wb - t_len - r0)
            dst[0, r0:r0 + n, :] = src[0, r0 + t_len:r0 + t_len + n, :]
            r0 += n
        dst[0, wb - t_len:wb, :] = new[0, 0:t_len, :]

    sub = lax.broadcasted_iota(jnp.int32, (SUBLANES, bw), 0)
    lane_head = lax.broadcasted_iota(jnp.int32, (SUBLANES, bw), 1) // HD_B
    own = sub == lane_head
    qv = q_ref[0]
    q_rows = jnp.concatenate(
        [jnp.where(own, jnp.broadcast_to(qv[t:t + 1, :], (SUBLANES, bw)), 0.0) for t in range(t_len)],
        axis=0).astype(BF16)

    s_c = lax.dot_general(q_rows, ck_ref[0].astype(BF16), NT_DIMS, preferred_element_type=F32) + bc_ref[...]
    s_n = lax.dot_general(q_rows, kn_ref[0].astype(BF16), NT_DIMS, preferred_element_type=F32) + bn_ref[...]
    m = jnp.maximum(jnp.max(s_c, axis=-1, keepdims=True), jnp.max(s_n, axis=-1, keepdims=True))
    p_c = jnp.exp(s_c - m)
    p_n = jnp.exp(s_n - m)
    den = jnp.sum(p_c, axis=-1, keepdims=True) + jnp.sum(p_n, axis=-1, keepdims=True)
    o = jnp.dot(p_c.astype(BF16), cv_ref[0].astype(BF16), preferred_element_type=F32)
    o = o + jnp.dot(p_n.astype(BF16), vn_ref[0].astype(BF16), preferred_element_type=F32)
    o = o / den
    rows = [jnp.sum(jnp.where(own, o[t * nh:(t + 1) * nh, :], 0.0), axis=0, keepdims=True)
            for t in range(t_len)]
    rows.append(jnp.zeros((SUBLANES - t_len, bw), F32))
    o_ref[0] = jnp.concatenate(rows, axis=0)


def _attn_sample(q, k_new, v_new, cache_k, cache_v, *, t_len):
    bsz, wb, bw = cache_k.shape
    nh = bw // HD_B
    t = np.repeat(np.arange(t_len), nh)[:, None]
    bias_c = jnp.asarray(_log_multiplicity(wb + t - np.arange(wb)[None, :]))
    tn = np.arange(SUBLANES)[None, :]
    bias_n = jnp.asarray(np.where(tn < t_len, _log_multiplicity(t - tn), NEG_INF).astype(np.float32))
    small = pl.BlockSpec((1, SUBLANES, bw), lambda b: (b, 0, 0))
    big = pl.BlockSpec((1, wb, bw), lambda b: (b, 0, 0))
    return pl.pallas_call(
        functools.partial(_sattn_kernel, t_len=t_len),
        grid=(bsz,),
        in_specs=[small, small, small, big, big, _const_spec(bias_c.shape), _const_spec(bias_n.shape)],
        out_specs=[small, big, big],
        out_shape=[jax.ShapeDtypeStruct((bsz, SUBLANES, bw), F32),
                   jax.ShapeDtypeStruct(cache_k.shape, cache_k.dtype),
                   jax.ShapeDtypeStruct(cache_v.shape, cache_v.dtype)],
        compiler_params=_params(("parallel",)),
        name="sattn",
    )(q, k_new, v_new, cache_k, cache_v, bias_c, bias_n)


def _post_kernel(x_ref, oa_ref, ob_ref, p_ref, cin_ref,
                 wo_ref, gf_ref, wg_ref, wu_ref, cw_ref, cb_ref, wd_ref, gp_ref, wpg_ref, wpp_ref, gl_ref,
                 y_ref, cout_ref, prev_ref, *, stride, ff_chunk):
    ts = x_ref.shape[1]
    d_ff = wg_ref.shape[1]
    aw = oa_ref.shape[2]
    n_prev = prev_ref.shape[0]

    @pl.when(pl.program_id(1) == 0)
    def _():
        prev_ref[...] = cin_ref[0]

    h = x_ref[0] + (jnp.dot(oa_ref[0], wo_ref[0:aw, :], preferred_element_type=F32)
                    + jnp.dot(ob_ref[0], wo_ref[aw:, :], preferred_element_type=F32))
    hn = _rmsnorm(h, gf_ref[...]).astype(BF16)

    sub = lax.broadcasted_iota(jnp.int32, (SUBLANES, 1), 0)
    for c0 in range(0, d_ff, ff_chunk):
        cs = slice(c0, min(c0 + ff_chunk, d_ff))
        u = jnp.dot(hn, wg_ref[:, cs], preferred_element_type=F32)
        prev = prev_ref[:, cs]
        if stride % SUBLANES == 0:
            u1 = jnp.concatenate([prev[stride:], u[:ts - stride]], axis=0)
            u2 = jnp.concatenate([prev, u[:ts - 2 * stride]], axis=0)
        else:
            assert stride == 1 and n_prev == SUBLANES
            r1, r2 = pltpu.roll(u, 1, 0), pltpu.roll(u, 2, 0)
            head1 = jnp.where(sub < 1, pltpu.roll(prev, 1, 0), r1[:SUBLANES])
            head2 = jnp.where(sub < 2, pltpu.roll(prev, 2, 0), r2[:SUBLANES])
            u1 = jnp.concatenate([head1, r1[SUBLANES:]], axis=0)
            u2 = jnp.concatenate([head2, r2[SUBLANES:]], axis=0)
        prev_ref[:, cs] = u[ts - n_prev:]
        conv = cb_ref[:, cs] + cw_ref[0:1, cs] * u2 + cw_ref[1:2, cs] * u1 + cw_ref[2:3, cs] * u
        up = jnp.dot(hn, wu_ref[:, cs], preferred_element_type=F32)
        act = (_silu(conv) * up).astype(BF16)
        down = jnp.dot(act, wd_ref[cs, :], preferred_element_type=F32)
        if c0 == 0:
            y_ref[0] = down
        else:
            y_ref[0] += down
    cout_ref[0] = prev_ref[...]

    h = h + y_ref[0]
    gate = _sigmoid(jnp.dot(_rmsnorm(h, gp_ref[...]).astype(BF16), wpg_ref[...], preferred_element_type=F32))
    h = h + gate * jnp.dot(p_ref[0].astype(BF16), wpp_ref[...], preferred_element_type=F32)
    y_ref[0] = _rmsnorm(h, gl_ref[...])


def _post(x, oa, ob, p, conv_in, w_o, g_ffn, w_gate, w_up, conv_w, conv_b, w_down, g_ple, w_pg, w_pp,
          g_fin, *, ts, stride):
    bsz, t_len, d = x.shape
    d_ff = w_gate.shape[1]
    n_prev = conv_in.shape[1]
    ff_chunk = -(-d_ff // (2 * MXU_WIDTH)) * MXU_WIDTH
    assert d_ff % MXU_WIDTH == 0 and t_len % ts == 0 and ts >= 2 * stride
    tok = lambda b, j: (b, j, 0)
    seq = lambda b, j: (b, 0, 0)
    row = lambda a: a.reshape(1, -1)
    weights = [w_o, row(g_ffn), w_gate, w_up, conv_w, row(conv_b), w_down, row(g_ple), w_pg, w_pp, row(g_fin)]
    return pl.pallas_call(
        functools.partial(_post_kernel, stride=stride, ff_chunk=ff_chunk),
        grid=(bsz, t_len // ts),
        in_specs=[pl.BlockSpec((1, ts, d), tok),
                  pl.BlockSpec((1, ts, oa.shape[2]), tok),
                  pl.BlockSpec((1, ts, ob.shape[2]), tok),
                  pl.BlockSpec((1, ts, p.shape[2]), tok),
                  pl.BlockSpec((1, n_prev, d_ff), seq)] + [_const_spec(w.shape) for w in weights],
        out_specs=[pl.BlockSpec((1, ts, d), tok), pl.BlockSpec((1, n_prev, d_ff), seq)],
        out_shape=[jax.ShapeDtypeStruct((bsz, t_len, d), F32),
                   jax.ShapeDtypeStruct((bsz, n_prev, d_ff), F32)],
        scratch_shapes=[pltpu.VMEM((n_prev, d_ff), F32)],
        compiler_params=_params(("parallel", "arbitrary")),
        name="post",
    )(x, oa, ob, p, conv_in, *weights)


PROMPT_TILE = 512


def kernel(x_prompt, x_sample, state_hgrn, cache_win_k, cache_win_v, state_ffn_conv, p_prompt, p_sample,
           norm_attn_g, w_in, hgrn_lb_logits, hgrn_onorm_g, w_o, norm_ffn_g, w_gate, w_up, conv_w, conv_b,
           w_down, norm_ple_g, w_ple_gate, w_ple_proj, norm_final_g):
    depth = w_in.shape[0]
    assert depth == 1, "single-layer step"
    i = 0
    bsz, s_len, d = x_prompt.shape
    dbs, t_dec, _ = x_sample.shape
    nh_a, hd = state_hgrn.shape[2], state_hgrn.shape[3]
    wb, nh_b = cache_win_k.shape[2], cache_win_k.shape[3]
    bw = nh_b * HD_B
    d_ff = w_gate.shape[2]
    past_len = PAST_LEN
    assert wb == min(WIN_MAX, past_len) and s_len <= WIN_MAX and t_dec <= SUBLANES
    n_conv = CONV_W - 1

    w_in16, w_o16 = w_in[i].astype(BF16), w_o[i].astype(BF16)
    w_gate16, w_up16, w_down16 = w_gate[i].astype(BF16), w_up[i].astype(BF16), w_down[i].astype(BF16)
    w_pg16, w_pp16 = w_ple_gate[i].astype(BF16), w_ple_proj[i].astype(BF16)
    post_w = (w_o16, norm_ffn_g[i], w_gate16, w_up16, conv_w[i], conv_b[i], w_down16, norm_ple_g[i],
              w_pg16, w_pp16, norm_final_g)

    cos_p, sin_p = _rotary_tables(jnp.arange(s_len, dtype=F32))
    a_p, qb_p, kb_p, vb_p = _proj(x_prompt, norm_attn_g[i], w_in16, hgrn_lb_logits, cos_p, sin_p,
                                  layer=i, ts=PROMPT_TILE)
    oa_p, s_p = _gla(a_p, jnp.zeros((bsz, nh_a, hd, hd), F32), hgrn_onorm_g[i], chunk=min(HGRN_CHUNK, s_len))
    ob_p = _attn_prompt(qb_p, kb_p, vb_p)
    y_p, conv_p = _post(x_prompt, oa_p, ob_p, p_prompt[i], jnp.zeros((bsz, SUBLANES, d_ff), F32), *post_w,
                        ts=PROMPT_TILE, stride=1)
    keep = min(WIN_MAX, s_len)

    n_tok = dbs * t_dec
    x_s = x_sample.transpose(1, 0, 2).reshape(1, n_tok, d)
    pos_s = jnp.repeat(float(past_len) + jnp.arange(t_dec, dtype=F32), dbs)
    cos_s, sin_s = _rotary_tables(pos_s)
    a_s, qb_s, kb_s, vb_s = _proj(x_s, norm_attn_g[i], w_in16, hgrn_lb_logits, cos_s, sin_s,
                                  layer=i, ts=n_tok)

    def per_seq(a, fill=0.0):
        a = a.reshape(t_dec, dbs, a.shape[-1]).transpose(1, 0, 2).astype(F32)
        return jnp.pad(a, ((0, 0), (0, SUBLANES - t_dec), (0, 0)), constant_values=fill)

    aw = nh_a * hd
    a_s = jnp.concatenate([per_seq(a_s[..., :aw]), per_seq(a_s[..., aw:2 * aw], 1.0),
                           per_seq(a_s[..., 2 * aw:])], axis=-1)
    oa_s, s_s = _gla(a_s, state_hgrn[i].astype(F32), hgrn_onorm_g[i], chunk=SUBLANES)
    ob_s, k_s, v_s = _attn_sample(per_seq(qb_s), per_seq(kb_s), per_seq(vb_s),
                                  cache_win_k[i].reshape(dbs, wb, bw), cache_win_v[i].reshape(dbs, wb, bw),
                                  t_len=t_dec)

    def tile_rows(a):
        return a[:, :t_dec].transpose(1, 0, 2).reshape(1, n_tok, a.shape[-1])

    conv_in_s = state_ffn_conv[i].transpose(1, 0, 2).reshape(1, n_conv * dbs, d_ff)
    p_s = p_sample[i].transpose(1, 0, 2).reshape(1, n_tok, -1)
    y_s, conv_s = _post(x_s, tile_rows(oa_s), tile_rows(ob_s).astype(BF16), p_s, conv_in_s, *post_w,
                        ts=n_tok, stride=dbs)

    return (y_p,
            y_s.reshape(t_dec, dbs, d).transpose(1, 0, 2),
            s_p[None],
            kb_p[:, s_len - keep:].reshape(1, bsz, keep, nh_b, HD_B),
            vb_p[:, s_len - keep:].reshape(1, bsz, keep, nh_b, HD_B),
            conv_p[None, :, SUBLANES - n_conv:],
            s_s[None],
            k_s.reshape(1, dbs, wb, nh_b, HD_B),
            v_s.reshape(1, dbs, wb, nh_b, HD_B),
            conv_s.reshape(n_conv, dbs, d_ff).transpose(1, 0, 2)[None])
```

```python
import functools

import numpy as np
import jax
import jax.numpy as jnp
from jax import lax
from jax.experimental import pallas as pl
from jax.experimental.pallas import tpu as pltpu

F32 = jnp.float32
BF16 = jnp.bfloat16

HGRN_HEAD_DIM = 128
HD_B = 64
ROT_DIM = HD_B // 4
ROPE_THETA = 500000.0
DILATIONS = ((128, 1), (512, 4), (2048, 16))
WIN_MAX = 2048
PAST_LEN = 16384
HGRN_CHUNK = 64
CONV_W = 3
EPS = 1e-6
NEG_INF = -1e30

SUBLANES = 8
LANES = 128
MXU_WIDTH = 256
VMEM_LIMIT_BYTES = 56 * 1024 * 1024

NT_DIMS = (((1,), (1,)), ((), ()))
TN_DIMS = (((0,), (0,)), ((), ()))


def _rmsnorm(x, g):
    return x * lax.rsqrt(jnp.mean(x * x, axis=-1, keepdims=True) + EPS) * g


def _sigmoid(x):
    return 1.0 / (1.0 + jnp.exp(-x))


def _silu(x):
    return x * _sigmoid(x)


def _const_spec(shape):
    nd = len(shape)
    return pl.BlockSpec(shape, lambda *_: (0,) * nd, pipeline_mode=pl.Buffered(1))


def _params(semantics):
    return pltpu.CompilerParams(dimension_semantics=semantics, vmem_limit_bytes=VMEM_LIMIT_BYTES)


def _proj_kernel(x_ref, g_ref, w_ref, lbl_ref, cos_ref, sin_ref,
                 a_ref, qb_ref, kb_ref, vb_ref, *, layer, a_width, b_width):
    x = x_ref[0]
    a = _rmsnorm(x, g_ref[...]).astype(BF16)

    def pj(j0, width):
        return jnp.dot(a, w_ref[:, j0:j0 + width], preferred_element_type=F32)

    lbl = lbl_ref[...]
    e = jnp.exp(lbl - jnp.max(lbl, axis=0, keepdims=True))
    lb = jnp.sum(e[:layer + 1], axis=0, keepdims=True) / jnp.sum(e, axis=0, keepdims=True)

    aw = a_width
    a_ref[0, :, 0:aw] = _silu(pj(0, aw)) * (HGRN_HEAD_DIM ** -0.5)
    a_ref[0, :, aw:2 * aw] = lb + (1.0 - lb) * _sigmoid(pj(aw, aw))
    a_ref[0, :, 2 * aw:3 * aw] = pj(2 * aw, aw)
    a_ref[0, :, 3 * aw:4 * aw] = _silu(pj(3 * aw, aw))

    lane = lax.broadcasted_iota(jnp.int32, (1, LANES), 1) % HD_B
    first_half = lane < (ROT_DIM // 2)
    cos = cos_ref[...]
    sin = sin_ref[...]

    def rotary(t):
        up = pltpu.roll(t, LANES - ROT_DIM // 2, 1)
        dn = pltpu.roll(t, ROT_DIM // 2, 1)
        return t * cos + jnp.where(first_half, up, dn) * sin

    bw = b_width
    q = pj(4 * aw, bw)
    k = pj(4 * aw + bw, bw)
    for c in range(bw // LANES):
        sl = slice(c * LANES, (c + 1) * LANES)
        qb_ref[0, :, sl] = (rotary(q[:, sl]) * (HD_B ** -0.5)).astype(qb_ref.dtype)
        kb_ref[0, :, sl] = rotary(k[:, sl])
    vb_ref[0] = pj(4 * aw + 2 * bw, bw)


def _proj(x, g, w_in, lb_logits, cos_t, sin_t, *, layer, ts):
    bsz, t_len, d = x.shape
    a_width = lb_logits.shape[1]
    b_width = (w_in.shape[1] - 4 * a_width) // 3
    nt = t_len // ts
    tok = lambda j, b: (b, j, 0)
    return pl.pallas_call(
        functools.partial(_proj_kernel, layer=layer, a_width=a_width, b_width=b_width),
        grid=(nt, bsz),
        in_specs=[
            pl.BlockSpec((1, ts, d), tok),
            _const_spec((1, d)),
            _const_spec(w_in.shape),
            _const_spec(lb_logits.shape),
            pl.BlockSpec((ts, LANES), lambda j, b: (j, 0)),
            pl.BlockSpec((ts, LANES), lambda j, b: (j, 0)),
        ],
        out_specs=[
            pl.BlockSpec((1, ts, 4 * a_width), tok),
            pl.BlockSpec((1, ts, b_width), tok),
            pl.BlockSpec((1, ts, b_width), tok),
            pl.BlockSpec((1, ts, b_width), tok),
        ],
        out_shape=[
            jax.ShapeDtypeStruct((bsz, t_len, 4 * a_width), F32),
            jax.ShapeDtypeStruct((bsz, t_len, b_width), BF16),
            jax.ShapeDtypeStruct((bsz, t_len, b_width), F32),
            jax.ShapeDtypeStruct((bsz, t_len, b_width), F32),
        ],
        compiler_params=_params(("parallel", "parallel")),
        name="proj",
    )(x, g.reshape(1, d), w_in, lb_logits, cos_t, sin_t)


def _rotary_tables(pos):
    half = ROT_DIM // 2
    inv_freq = jnp.power(ROPE_THETA, -jnp.arange(half, dtype=F32) * (2.0 / ROT_DIM))
    ang = pos[:, None] * inv_freq[None, :]
    cos, sin = jnp.cos(ang), jnp.sin(ang)
    t_len = pos.shape[0]
    ones = jnp.ones((t_len, HD_B - ROT_DIM), F32)
    cos_h = jnp.concatenate([cos, cos, ones], axis=1)
    sin_h = jnp.concatenate([-sin, sin, 0.0 * ones], axis=1)
    reps = LANES // HD_B
    return jnp.tile(cos_h, (1, reps)), jnp.tile(sin_h, (1, reps))


def _boundary_rows(b_ref, chunk, half):
    sub = lax.broadcasted_iota(jnp.int32, (SUBLANES, 1), 0)
    pieces = []
    for grp in range(chunk // SUBLANES):
        base = grp * SUBLANES
        if 2 * half >= SUBLANES:
            r = (base // (2 * half)) * (2 * half) + half - 1
            piece = jnp.broadcast_to(b_ref[r:r + 1, :], (SUBLANES, LANES))
        else:
            piece = None
            for blk in range(SUBLANES // (2 * half)):
                r = base + blk * 2 * half + half - 1
                row = jnp.broadcast_to(b_ref[r:r + 1, :], (SUBLANES, LANES))
                piece = row if piece is None else jnp.where(sub >= blk * 2 * half, row, piece)
        pieces.append(piece)
    return pieces[0] if len(pieces) == 1 else jnp.concatenate(pieces, axis=0)


def _gla_kernel(q_ref, f_ref, v_ref, gs_ref, s0_ref, on_ref, o_ref, s_ref, st_ref, b_ref, *, chunk, group):
    n_seq, t_len = q_ref.shape[0], q_ref.shape[1]
    for sq in range(n_seq):
        st_ref[sq] = s0_ref[sq, 0].T

    row = lax.broadcasted_iota(jnp.int32, (chunk, chunk), 0)
    col = lax.broadcasted_iota(jnp.int32, (chunk, chunk), 1)
    differ = row ^ col
    halves = []
    h = 1
    while h < chunk:
        halves.append(h)
        h *= 2
    level = jnp.where(row < col, -1, 0)
    for n, half in enumerate(halves):
        level = jnp.where((row > col) & (differ >= half), n + 1, level)
    rows1 = lax.broadcasted_iota(jnp.int32, (chunk, 1), 0)
    onorm_g = on_ref[...]

    def one_chunk(sq, slot, r0, st):
        q = q_ref[sq, pl.ds(r0, chunk), :]
        fg = f_ref[sq, pl.ds(r0, chunk), :]
        v16 = v_ref[sq, pl.ds(r0, chunk), :].astype(BF16)
        g = jnp.log2(fg)
        k = 1.0 - fg

        b = g
        d = 1
        while d < chunk:
            b = b + jnp.where(rows1 >= d, pltpu.roll(b, d, 0), 0.0)
            d *= 2
        bs_ref = b_ref.at[slot]
        bs_ref[...] = b

        a_mat = jnp.where(level == 0,
                          lax.dot_general(q.astype(BF16), k.astype(BF16), NT_DIMS,
                                          preferred_element_type=F32), 0.0)
        for n, half in enumerate(halves):
            mid = _boundary_rows(bs_ref, chunk, half)
            e = jnp.exp2(-jnp.abs(b - mid))
            lvl = lax.dot_general((q * e).astype(BF16), (k * e).astype(BF16), NT_DIMS,
                                  preferred_element_type=F32)
            a_mat = jnp.where(level == n + 1, lvl, a_mat)

        o = lax.dot_general((q * jnp.exp2(b)).astype(BF16), st.astype(BF16), NT_DIMS,
                            preferred_element_type=F32)
        o = o + jnp.dot(a_mat.astype(BF16), v16, preferred_element_type=F32)
        o = _rmsnorm(o, onorm_g) * gs_ref[sq, pl.ds(r0, chunk), :]
        o_ref[sq, pl.ds(r0, chunk), :] = o.astype(o_ref.dtype)

        b_last = bs_ref[chunk - 1:chunk, :]
        kd = (k * jnp.exp2(b_last - b)).astype(BF16)
        return st * jnp.exp2(b_last) + lax.dot_general(v16, kd, TN_DIMS, preferred_element_type=F32)

    def body(c, carry):
        for sq in range(n_seq):
            st = st_ref[sq]
            for u in range(group):
                r0 = pl.multiple_of((c * group + u) * chunk, chunk)
                st = one_chunk(sq, sq * group + u, r0, st)
            st_ref[sq] = st
        return carry

    lax.fori_loop(0, t_len // (chunk * group), body, 0)
    for sq in range(n_seq):
        s_ref[sq, 0] = st_ref[sq].T


def _gla(a_part, s0, onorm_g, *, chunk, group, n_seq):
    bsz, t_len, a4 = a_part.shape
    nh = s0.shape[1]
    hd = HGRN_HEAD_DIM
    assert bsz % n_seq == 0 and t_len % (chunk * group) == 0
    col = lambda part: pl.BlockSpec((n_seq, t_len, hd), lambda b, h, part=part: (b, 0, part * nh + h))
    return pl.pallas_call(
        functools.partial(_gla_kernel, chunk=chunk, group=group),
        grid=(bsz // n_seq, nh),
        in_specs=[col(0), col(1), col(2), col(3),
                  pl.BlockSpec((n_seq, 1, hd, hd), lambda b, h: (b, h, 0, 0)),
                  pl.BlockSpec((1, hd), lambda b, h: (0, h))],
        out_specs=[pl.BlockSpec((n_seq, t_len, hd), lambda b, h: (b, 0, h)),
                   pl.BlockSpec((n_seq, 1, hd, hd), lambda b, h: (b, h, 0, 0))],
        out_shape=[jax.ShapeDtypeStruct((bsz, t_len, a4 // 4), BF16),
                   jax.ShapeDtypeStruct(s0.shape, F32)],
        scratch_shapes=[pltpu.VMEM((n_seq, hd, hd), F32), pltpu.VMEM((n_seq * group, chunk, hd), F32)],
        compiler_params=_params(("parallel", "parallel")),
        name="gla",
    )(a_part, a_part, a_part, a_part, s0, onorm_g.reshape(1, -1))


def _log_multiplicity(dist):
    count = np.zeros(dist.shape, np.float64)
    for window, dil in DILATIONS:
        count += (dist >= 0) & (dist <= window) & (dist % dil == 0)
    return np.where(count > 0, np.log(np.maximum(count, 1.0)), NEG_INF).astype(np.float32)


ATTN_TILE = 256


def _attn_kernel(q_ref, k_ref, v_ref, bias_ref, o_ref, k16_ref, v16_ref):
    s_len = q_ref.shape[1]
    n_tiles = s_len // ATTN_TILE
    k16_ref[...] = k_ref[0].astype(BF16)
    v16_ref[...] = v_ref[0].astype(BF16)
    lane = lax.broadcasted_iota(jnp.int32, (1, LANES), 1)
    heads = [lane < HD_B, lane >= HD_B]
    for i in range(n_tiles):
        rows = slice(i * ATTN_TILE, (i + 1) * ATTN_TILE)
        n_keys = (i + 1) * ATTN_TILE
        q = q_ref[0, rows, :]
        outs = []
        for head in heads:
            s = lax.dot_general(jnp.where(head, q, jnp.zeros_like(q)), k16_ref[0:n_keys, :], NT_DIMS,
                                preferred_element_type=F32)
            s = s + bias_ref[:, s_len - n_keys:s_len]
            p = jnp.exp(s - jnp.max(s, axis=-1, keepdims=True))
            den = jnp.sum(p, axis=-1, keepdims=True)
            outs.append(jnp.dot(p.astype(BF16), v16_ref[0:n_keys, :], preferred_element_type=F32) / den)
        o_ref[0, rows, :] = jnp.where(heads[0], outs[0], outs[1]).astype(o_ref.dtype)


def _attn_prompt(q, k, v):
    bsz, s_len, bw = q.shape
    assert s_len % ATTN_TILE == 0 and bw % LANES == 0
    r = np.arange(ATTN_TILE)[:, None]
    c = np.arange(s_len)[None, :]
    bias = jnp.asarray(_log_multiplicity(r - (c - (s_len - ATTN_TILE))))
    blk = pl.BlockSpec((1, s_len, LANES), lambda b, h: (b, 0, h))
    return pl.pallas_call(
        _attn_kernel,
        grid=(bsz, bw // LANES),
        in_specs=[blk, blk, blk, _const_spec(bias.shape)],
        out_specs=blk,
        out_shape=jax.ShapeDtypeStruct((bsz, s_len, bw), BF16),
        scratch_shapes=[pltpu.VMEM((s_len, LANES), BF16), pltpu.VMEM((s_len, LANES), BF16)],
        compiler_params=_params(("parallel", "parallel")),
        name="attn",
    )(q, k, v, bias)


SHIFT_ROWS = 256


def _sattn_kernel(q_ref, kn_ref, vn_ref, ck_ref, cv_ref, bc_ref, bn_ref,
                  o_ref, ok_ref, ov_ref, *, t_len):
    wb = ck_ref.shape[1]
    bw = ck_ref.shape[2]
    nh = bw // HD_B
    assert nh == SUBLANES

    for src, new, dst in ((ck_ref, kn_ref, ok_ref), (cv_ref, vn_ref, ov_ref)):
        r0 = 0
        while r0 < wb - t_len:
            n = min(SHIFT_ROWS, wb - t_len - r0)
            dst[0, r0:r0 + n, :] = src[0, r0 + t_len:r0 + t_len + n, :]
            r0 += n
        dst[0, wb - t_len:wb, :] = new[0, 0:t_len, :]

    sub = lax.broadcasted_iota(jnp.int32, (SUBLANES, bw), 0)
    lane_head = lax.broadcasted_iota(jnp.int32, (SUBLANES, bw), 1) // HD_B
    own = sub == lane_head
    qv = q_ref[0]
    q_rows = jnp.concatenate(
        [jnp.where(own, jnp.broadcast_to(qv[t:t + 1, :], (SUBLANES, bw)), 0.0) for t in range(t_len)],
        axis=0).astype(BF16)

    s_c = lax.dot_general(q_rows, ck_ref[0].astype(BF16), NT_DIMS, preferred_element_type=F32) + bc_ref[...]
    s_n = lax.dot_general(q_rows, kn_ref[0].astype(BF16), NT_DIMS, preferred_element_type=F32) + bn_ref[...]
    m = jnp.maximum(jnp.max(s_c, axis=-1, keepdims=True), jnp.max(s_n, axis=-1, keepdims=True))
    p_c = jnp.exp(s_c - m)
    p_n = jnp.exp(s_n - m)
    den = jnp.sum(p_c, axis=-1, keepdims=True) + jnp.sum(p_n, axis=-1, keepdims=True)
    o = jnp.dot(p_c.astype(BF16), cv_ref[0].astype(BF16), preferred_element_type=F32)
    o = o + jnp.dot(p_n.astype(BF16), vn_ref[0].astype(BF16), preferred_element_type=F32)
    o = o / den
    rows = [jnp.sum(jnp.where(own, o[t * nh:(t + 1) * nh, :], 0.0), axis=0, keepdims=True)
            for t in range(t_len)]
    rows.append(jnp.zeros((SUBLANES - t_len, bw), F32))
    o_ref[0] = jnp.concatenate(rows, axis=0)


def _attn_sample(q, k_new, v_new, cache_k, cache_v, *, t_len):
    bsz, wb, bw = cache_k.shape
    nh = bw // HD_B
    t = np.repeat(np.arange(t_len), nh)[:, None]
    bias_c = jnp.asarray(_log_multiplicity(wb + t - np.arange(wb)[None, :]))
    tn = np.arange(SUBLANES)[None, :]
    bias_n = jnp.asarray(np.where(tn < t_len, _log_multiplicity(t - tn), NEG_INF).astype(np.float32))
    small = pl.BlockSpec((1, SUBLANES, bw), lambda b: (b, 0, 0))
    big = pl.BlockSpec((1, wb, bw), lambda b: (b, 0, 0))
    return pl.pallas_call(
        functools.partial(_sattn_kernel, t_len=t_len),
        grid=(bsz,),
        in_specs=[small, small, small, big, big, _const_spec(bias_c.shape), _const_spec(bias_n.shape)],
        out_specs=[small, big, big],
        out_shape=[jax.ShapeDtypeStruct((bsz, SUBLANES, bw), F32),
                   jax.ShapeDtypeStruct(cache_k.shape, cache_k.dtype),
                   jax.ShapeDtypeStruct(cache_v.shape, cache_v.dtype)],
        compiler_params=_params(("parallel",)),
        name="sattn",
    )(q, k_new, v_new, cache_k, cache_v, bias_c, bias_n)


def _post_kernel(x_ref, oa_ref, ob_ref, p_ref, cin_ref,
                 wo_ref, gf_ref, wg_ref, wu_ref, cw_ref, cb_ref, wd_ref, gp_ref, wpg_ref, wpp_ref, gl_ref,
                 y_ref, cout_ref, prev_ref, *, stride, ff_chunk):
    ts = x_ref.shape[1]
    d_ff = wg_ref.shape[1]
    aw = oa_ref.shape[2]
    n_prev = prev_ref.shape[0]

    @pl.when(pl.program_id(1) == 0)
    def _():
        prev_ref[...] = cin_ref[0]

    h = x_ref[0] + (jnp.dot(oa_ref[0], wo_ref[0:aw, :], preferred_element_type=F32)
                    + jnp.dot(ob_ref[0], wo_ref[aw:, :], preferred_element_type=F32))
    hn = _rmsnorm(h, gf_ref[...]).astype(BF16)

    sub = lax.broadcasted_iota(jnp.int32, (SUBLANES, 1), 0)
    for c0 in range(0, d_ff, ff_chunk):
        cs = slice(c0, min(c0 + ff_chunk, d_ff))
        u = jnp.dot(hn, wg_ref[:, cs], preferred_element_type=F32)
        prev = prev_ref[:, cs]
        if stride % SUBLANES == 0:
            u1 = jnp.concatenate([prev[stride:], u[:ts - stride]], axis=0)
            u2 = jnp.concatenate([prev, u[:ts - 2 * stride]], axis=0)
        else:
            assert stride == 1 and n_prev == SUBLANES
            r1, r2 = pltpu.roll(u, 1, 0), pltpu.roll(u, 2, 0)
            head1 = jnp.where(sub < 1, pltpu.roll(prev, 1, 0), r1[:SUBLANES])
            head2 = jnp.where(sub < 2, pltpu.roll(prev, 2, 0), r2[:SUBLANES])
            u1 = jnp.concatenate([head1, r1[SUBLANES:]], axis=0)
            u2 = jnp.concatenate([head2, r2[SUBLANES:]], axis=0)
        prev_ref[:, cs] = u[ts - n_prev:]
        conv = cb_ref[:, cs] + cw_ref[0:1, cs] * u2 + cw_ref[1:2, cs] * u1 + cw_ref[2:3, cs] * u
        up = jnp.dot(hn, wu_ref[:, cs], preferred_element_type=F32)
        act = (_silu(conv) * up).astype(BF16)
        down = jnp.dot(act, wd_ref[cs, :], preferred_element_type=F32)
        if c0 == 0:
            y_ref[0] = down
        else:
            y_ref[0] += down
    cout_ref[0] = prev_ref[...]

    h = h + y_ref[0]
    gate = _sigmoid(jnp.dot(_rmsnorm(h, gp_ref[...]).astype(BF16), wpg_ref[...], preferred_element_type=F32))
    h = h + gate * jnp.dot(p_ref[0].astype(BF16), wpp_ref[...], preferred_element_type=F32)
    y_ref[0] = _rmsnorm(h, gl_ref[...])


def _post(x, oa, ob, p, conv_in, w_o, g_ffn, w_gate, w_up, conv_w, conv_b, w_down, g_ple, w_pg, w_pp,
          g_fin, *, ts, stride):
    bsz, t_len, d = x.shape
    d_ff = w_gate.shape[1]
    n_prev = conv_in.shape[1]
    ff_chunk = -(-d_ff // (2 * MXU_WIDTH)) * MXU_WIDTH
    assert d_ff % MXU_WIDTH == 0 and t_len % ts == 0 and ts >= 2 * stride
    tok = lambda b, j: (b, j, 0)
    seq = lambda b, j: (b, 0, 0)
    row = lambda a: a.reshape(1, -1)
    weights = [w_o, row(g_ffn), w_gate, w_up, conv_w, row(conv_b), w_down, row(g_ple), w_pg, w_pp, row(g_fin)]
    return pl.pallas_call(
        functools.partial(_post_kernel, stride=stride, ff_chunk=ff_chunk),
        grid=(bsz, t_len // ts),
        in_specs=[pl.BlockSpec((1, ts, d), tok),
                  pl.BlockSpec((1, ts, oa.shape[2]), tok),
                  pl.BlockSpec((1, ts, ob.shape[2]), tok),
                  pl.BlockSpec((1, ts, p.shape[2]), tok),
                  pl.BlockSpec((1, n_prev, d_ff), seq)] + [_const_spec(w.shape) for w in weights],
        out_specs=[pl.BlockSpec((1, ts, d), tok), pl.BlockSpec((1, n_prev, d_ff), seq)],
        out_shape=[jax.ShapeDtypeStruct((bsz, t_len, d), F32),
                   jax.ShapeDtypeStruct((bsz, n_prev, d_ff), F32)],
        scratch_shapes=[pltpu.VMEM((n_prev, d_ff), F32)],
        compiler_params=_params(("parallel", "arbitrary")),
        name="post",
    )(x, oa, ob, p, conv_in, *weights)


PROMPT_TILE = 512
GLA_GROUP = 8
SAMPLE_SEQS = 8


def kernel(x_prompt, x_sample, state_hgrn, cache_win_k, cache_win_v, state_ffn_conv, p_prompt, p_sample,
           norm_attn_g, w_in, hgrn_lb_logits, hgrn_onorm_g, w_o, norm_ffn_g, w_gate, w_up, conv_w, conv_b,
           w_down, norm_ple_g, w_ple_gate, w_ple_proj, norm_final_g):
    depth = w_in.shape[0]
    assert depth == 1, "single-layer step"
    i = 0
    bsz, s_len, d = x_prompt.shape
    dbs, t_dec, _ = x_sample.shape
    nh_a, hd = state_hgrn.shape[2], state_hgrn.shape[3]
    wb, nh_b = cache_win_k.shape[2], cache_win_k.shape[3]
    bw = nh_b * HD_B
    d_ff = w_gate.shape[2]
    past_len = PAST_LEN
    assert wb == min(WIN_MAX, past_len) and s_len <= WIN_MAX and t_dec <= SUBLANES
    n_conv = CONV_W - 1

    w_in16, w_o16 = w_in[i].astype(BF16), w_o[i].astype(BF16)
    w_gate16, w_up16, w_down16 = w_gate[i].astype(BF16), w_up[i].astype(BF16), w_down[i].astype(BF16)
    w_pg16, w_pp16 = w_ple_gate[i].astype(BF16), w_ple_proj[i].astype(BF16)
    post_w = (w_o16, norm_ffn_g[i], w_gate16, w_up16, conv_w[i], conv_b[i], w_down16, norm_ple_g[i],
              w_pg16, w_pp16, norm_final_g)

    cos_p, sin_p = _rotary_tables(jnp.arange(s_len, dtype=F32))
    a_p, qb_p, kb_p, vb_p = _proj(x_prompt, norm_attn_g[i], w_in16, hgrn_lb_logits, cos_p, sin_p,
                                  layer=i, ts=PROMPT_TILE)
    oa_p, s_p = _gla(a_p, jnp.zeros((bsz, nh_a, hd, hd), F32), hgrn_onorm_g[i], chunk=min(HGRN_CHUNK, s_len),
                    group=GLA_GROUP, n_seq=1)
    ob_p = _attn_prompt(qb_p, kb_p, vb_p)
    y_p, conv_p = _post(x_prompt, oa_p, ob_p, p_prompt[i], jnp.zeros((bsz, SUBLANES, d_ff), F32), *post_w,
                        ts=PROMPT_TILE, stride=1)
    keep = min(WIN_MAX, s_len)

    n_tok = dbs * t_dec
    x_s = x_sample.transpose(1, 0, 2).reshape(1, n_tok, d)
    pos_s = jnp.repeat(float(past_len) + jnp.arange(t_dec, dtype=F32), dbs)
    cos_s, sin_s = _rotary_tables(pos_s)
    a_s, qb_s, kb_s, vb_s = _proj(x_s, norm_attn_g[i], w_in16, hgrn_lb_logits, cos_s, sin_s,
                                  layer=i, ts=n_tok)

    def per_seq(a, fill=0.0):
        a = a.reshape(t_dec, dbs, a.shape[-1]).transpose(1, 0, 2).astype(F32)
        return jnp.pad(a, ((0, 0), (0, SUBLANES - t_dec), (0, 0)), constant_values=fill)

    aw = nh_a * hd
    a_s = jnp.concatenate([per_seq(a_s[..., :aw]), per_seq(a_s[..., aw:2 * aw], 1.0),
                           per_seq(a_s[..., 2 * aw:])], axis=-1)
    oa_s, s_s = _gla(a_s, state_hgrn[i].astype(F32), hgrn_onorm_g[i], chunk=SUBLANES, group=1,
                     n_seq=SAMPLE_SEQS)
    ob_s, k_s, v_s = _attn_sample(per_seq(qb_s), per_seq(kb_s), per_seq(vb_s),
                                  cache_win_k[i].reshape(dbs, wb, bw), cache_win_v[i].reshape(dbs, wb, bw),
                                  t_len=t_dec)

    def tile_rows(a):
        return a[:, :t_dec].transpose(1, 0, 2).reshape(1, n_tok, a.shape[-1])

    conv_in_s = state_ffn_conv[i].transpose(1, 0, 2).reshape(1, n_conv * dbs, d_ff)
    p_s = p_sample[i].transpose(1, 0, 2).reshape(1, n_tok, -1)
    y_s, conv_s = _post(x_s, tile_rows(oa_s), tile_rows(ob_s).astype(BF16), p_s, conv_in_s, *post_w,
                        ts=n_tok, stride=dbs)

    return (y_p,
            y_s.reshape(t_dec, dbs, d).transpose(1, 0, 2),
            s_p[None],
            kb_p[:, s_len - keep:].reshape(1, bsz, keep, nh_b, HD_B),
            vb_p[:, s_len - keep:].reshape(1, bsz, keep, nh_b, HD_B),
            conv_p[None, :, SUBLANES - n_conv:],
            s_s[None],
            k_s.reshape(1, dbs, wb, nh_b, HD_B),
            v_s.reshape(1, dbs, wb, nh_b, HD_B),
            conv_s.reshape(n_conv, dbs, d_ff).transpose(1, 0, 2)[None])
```

```python
import functools

import numpy as np
import jax
import jax.numpy as jnp
from jax import lax
from jax.experimental import pallas as pl
from jax.experimental.pallas import tpu as pltpu

F32 = jnp.float32
BF16 = jnp.bfloat16

HGRN_HEAD_DIM = 128
HD_B = 64
ROT_DIM = HD_B // 4
ROPE_THETA = 500000.0
DILATIONS = ((128, 1), (512, 4), (2048, 16))
WIN_MAX = 2048
PAST_LEN = 16384
HGRN_CHUNK = 64
CONV_W = 3
EPS = 1e-6
NEG_INF = -1e30

SUBLANES = 8
LANES = 128
MXU_WIDTH = 256
VMEM_LIMIT_BYTES = 56 * 1024 * 1024

NT_DIMS = (((1,), (1,)), ((), ()))
TN_DIMS = (((0,), (0,)), ((), ()))


def _rmsnorm(x, g):
    return x * lax.rsqrt(jnp.mean(x * x, axis=-1, keepdims=True) + EPS) * g


def _sigmoid(x):
    return 1.0 / (1.0 + jnp.exp(-x))


def _silu(x):
    return x * _sigmoid(x)


def _const_spec(shape):
    nd = len(shape)
    return pl.BlockSpec(shape, lambda *_: (0,) * nd, pipeline_mode=pl.Buffered(1))


def _params(semantics):
    return pltpu.CompilerParams(dimension_semantics=semantics, vmem_limit_bytes=VMEM_LIMIT_BYTES)


def _proj_kernel(x_ref, g_ref, w_ref, wkv_ref, lbl_ref, cos_ref, sin_ref, cost_ref, sint_ref,
                 a_ref, qb_ref, kt_ref, vt_ref, *, layer, a_width, b_width):
    x = x_ref[0]
    a = _rmsnorm(x, g_ref[...]).astype(BF16)

    def pj(j0, width):
        return jnp.dot(a, w_ref[:, j0:j0 + width], preferred_element_type=F32)

    lbl = lbl_ref[...]
    e = jnp.exp(lbl - jnp.max(lbl, axis=0, keepdims=True))
    lb = jnp.sum(e[:layer + 1], axis=0, keepdims=True) / jnp.sum(e, axis=0, keepdims=True)

    aw = a_width
    a_ref[0, :, 0:aw] = _silu(pj(0, aw)) * (HGRN_HEAD_DIM ** -0.5)
    a_ref[0, :, aw:2 * aw] = lb + (1.0 - lb) * _sigmoid(pj(aw, aw))
    a_ref[0, :, 2 * aw:3 * aw] = pj(2 * aw, aw)
    a_ref[0, :, 3 * aw:4 * aw] = _silu(pj(3 * aw, aw))

    half = ROT_DIM // 2
    lane = lax.broadcasted_iota(jnp.int32, (1, LANES), 1) % HD_B
    first_half = lane < half
    cos = cos_ref[...]
    sin = sin_ref[...]
    bw = b_width
    q = pj(4 * aw, bw)
    for c in range(bw // LANES):
        sl = slice(c * LANES, (c + 1) * LANES)
        t = q[:, sl]
        up = pltpu.roll(t, LANES - half, 1)
        dn = pltpu.roll(t, half, 1)
        rot = t * cos + jnp.where(first_half, up, dn) * sin
        qb_ref[0, :, sl] = (rot * (HD_B ** -0.5)).astype(qb_ref.dtype)

    assert half == SUBLANES
    kt = lax.dot_general(wkv_ref[0:bw, :], a, NT_DIMS, preferred_element_type=F32)
    cos_t = cost_ref[...]
    sin_t = sint_ref[...]
    for h in range(bw // HD_B):
        r = h * HD_B
        x1 = kt[r:r + half]
        x2 = kt[r + half:r + ROT_DIM]
        kt_ref[0, r:r + half, :] = x1 * cos_t - x2 * sin_t
        kt_ref[0, r + half:r + ROT_DIM, :] = x2 * cos_t + x1 * sin_t
        kt_ref[0, r + ROT_DIM:r + HD_B, :] = kt[r + ROT_DIM:r + HD_B]
    vt_ref[0] = lax.dot_general(wkv_ref[bw:2 * bw, :], a, NT_DIMS, preferred_element_type=F32)


def _proj(x, g, w_aq, w_kv_t, lb_logits, tables, *, layer, ts):
    bsz, t_len, d = x.shape
    a_width = lb_logits.shape[1]
    b_width = w_kv_t.shape[0] // 2
    cos_n, sin_n, cos_t, sin_t = tables
    nt = t_len // ts
    tok = lambda j, b: (b, j, 0)
    feat = lambda j, b: (b, 0, j)
    return pl.pallas_call(
        functools.partial(_proj_kernel, layer=layer, a_width=a_width, b_width=b_width),
        grid=(nt, bsz),
        in_specs=[
            pl.BlockSpec((1, ts, d), tok),
            _const_spec((1, d)),
            _const_spec(w_aq.shape),
            _const_spec(w_kv_t.shape),
            _const_spec(lb_logits.shape),
            pl.BlockSpec((ts, LANES), lambda j, b: (j, 0)),
            pl.BlockSpec((ts, LANES), lambda j, b: (j, 0)),
            pl.BlockSpec((ROT_DIM // 2, ts), lambda j, b: (0, j)),
            pl.BlockSpec((ROT_DIM // 2, ts), lambda j, b: (0, j)),
        ],
        out_specs=[
            pl.BlockSpec((1, ts, 4 * a_width), tok),
            pl.BlockSpec((1, ts, b_width), tok),
            pl.BlockSpec((1, b_width, ts), feat),
            pl.BlockSpec((1, b_width, ts), feat),
        ],
        out_shape=[
            jax.ShapeDtypeStruct((bsz, t_len, 4 * a_width), F32),
            jax.ShapeDtypeStruct((bsz, t_len, b_width), BF16),
            jax.ShapeDtypeStruct((bsz, b_width, t_len), F32),
            jax.ShapeDtypeStruct((bsz, b_width, t_len), F32),
        ],
        compiler_params=_params(("parallel", "parallel")),
        name="proj",
    )(x, g.reshape(1, d), w_aq, w_kv_t, lb_logits, cos_n, sin_n, cos_t, sin_t)


def _rotary_tables(pos):
    half = ROT_DIM // 2
    inv_freq = jnp.power(ROPE_THETA, -jnp.arange(half, dtype=F32) * (2.0 / ROT_DIM))
    ang = pos[:, None] * inv_freq[None, :]
    cos, sin = jnp.cos(ang), jnp.sin(ang)
    t_len = pos.shape[0]
    ones = jnp.ones((t_len, HD_B - ROT_DIM), F32)
    cos_h = jnp.concatenate([cos, cos, ones], axis=1)
    sin_h = jnp.concatenate([-sin, sin, 0.0 * ones], axis=1)
    reps = LANES // HD_B
    return jnp.tile(cos_h, (1, reps)), jnp.tile(sin_h, (1, reps)), cos.T, sin.T


def _boundary_rows(b_ref, chunk, half):
    sub = lax.broadcasted_iota(jnp.int32, (SUBLANES, 1), 0)
    pieces = []
    for grp in range(chunk // SUBLANES):
        base = grp * SUBLANES
        if 2 * half >= SUBLANES:
            r = (base // (2 * half)) * (2 * half) + half - 1
            piece = jnp.broadcast_to(b_ref[r:r + 1, :], (SUBLANES, LANES))
        else:
            piece = None
            for blk in range(SUBLANES // (2 * half)):
                r = base + blk * 2 * half + half - 1
                row = jnp.broadcast_to(b_ref[r:r + 1, :], (SUBLANES, LANES))
                piece = row if piece is None else jnp.where(sub >= blk * 2 * half, row, piece)
        pieces.append(piece)
    return pieces[0] if len(pieces) == 1 else jnp.concatenate(pieces, axis=0)


def _gla_kernel(q_ref, f_ref, v_ref, gs_ref, s0_ref, on_ref, o_ref, s_ref, st_ref, b_ref, *, chunk, group):
    n_seq, t_len = q_ref.shape[0], q_ref.shape[1]
    for sq in range(n_seq):
        st_ref[sq] = s0_ref[sq, 0].T

    row = lax.broadcasted_iota(jnp.int32, (chunk, chunk), 0)
    col = lax.broadcasted_iota(jnp.int32, (chunk, chunk), 1)
    differ = row ^ col
    halves = []
    h = 1
    while h < chunk:
        halves.append(h)
        h *= 2
    level = jnp.where(row < col, -1, 0)
    for n, half in enumerate(halves):
        level = jnp.where((row > col) & (differ >= half), n + 1, level)
    rows1 = lax.broadcasted_iota(jnp.int32, (chunk, 1), 0)
    onorm_g = on_ref[...]

    def one_chunk(sq, slot, r0, st):
        q = q_ref[sq, pl.ds(r0, chunk), :]
        fg = f_ref[sq, pl.ds(r0, chunk), :]
        v16 = v_ref[sq, pl.ds(r0, chunk), :].astype(BF16)
        g = jnp.log2(fg)
        k = 1.0 - fg

        b = g
        d = 1
        while d < chunk:
            b = b + jnp.where(rows1 >= d, pltpu.roll(b, d, 0), 0.0)
            d *= 2
        bs_ref = b_ref.at[slot]
        bs_ref[...] = b

        a_mat = jnp.where(level == 0,
                          lax.dot_general(q.astype(BF16), k.astype(BF16), NT_DIMS,
                                          preferred_element_type=F32), 0.0)
        for n, half in enumerate(halves):
            mid = _boundary_rows(bs_ref, chunk, half)
            e = jnp.exp2(-jnp.abs(b - mid))
            lvl = lax.dot_general((q * e).astype(BF16), (k * e).astype(BF16), NT_DIMS,
                                  preferred_element_type=F32)
            a_mat = jnp.where(level == n + 1, lvl, a_mat)

        o = lax.dot_general((q * jnp.exp2(b)).astype(BF16), st.astype(BF16), NT_DIMS,
                            preferred_element_type=F32)
        o = o + jnp.dot(a_mat.astype(BF16), v16, preferred_element_type=F32)
        o = _rmsnorm(o, onorm_g) * gs_ref[sq, pl.ds(r0, chunk), :]
        o_ref[sq, pl.ds(r0, chunk), :] = o.astype(o_ref.dtype)

        b_last = bs_ref[chunk - 1:chunk, :]
        kd = (k * jnp.exp2(b_last - b)).astype(BF16)
        return st * jnp.exp2(b_last) + lax.dot_general(v16, kd, TN_DIMS, preferred_element_type=F32)

    def body(c, carry):
        for sq in range(n_seq):
            st = st_ref[sq]
            for u in range(group):
                r0 = pl.multiple_of((c * group + u) * chunk, chunk)
                st = one_chunk(sq, sq * group + u, r0, st)
            st_ref[sq] = st
        return carry

    lax.fori_loop(0, t_len // (chunk * group), body, 0)
    for sq in range(n_seq):
        s_ref[sq, 0] = st_ref[sq].T


def _gla(a_part, s0, onorm_g, *, chunk, group, n_seq):
    bsz, t_len, a4 = a_part.shape
    nh = s0.shape[1]
    hd = HGRN_HEAD_DIM
    assert bsz % n_seq == 0 and t_len % (chunk * group) == 0
    col = lambda part: pl.BlockSpec((n_seq, t_len, hd), lambda b, h, part=part: (b, 0, part * nh + h))
    return pl.pallas_call(
        functools.partial(_gla_kernel, chunk=chunk, group=group),
        grid=(bsz // n_seq, nh),
        in_specs=[col(0), col(1), col(2), col(3),
                  pl.BlockSpec((n_seq, 1, hd, hd), lambda b, h: (b, h, 0, 0)),
                  pl.BlockSpec((1, hd), lambda b, h: (0, h))],
        out_specs=[pl.BlockSpec((n_seq, t_len, hd), lambda b, h: (b, 0, h)),
                   pl.BlockSpec((n_seq, 1, hd, hd), lambda b, h: (b, h, 0, 0))],
        out_shape=[jax.ShapeDtypeStruct((bsz, t_len, a4 // 4), BF16),
                   jax.ShapeDtypeStruct(s0.shape, F32)],
        scratch_shapes=[pltpu.VMEM((n_seq, hd, hd), F32), pltpu.VMEM((n_seq * group, chunk, hd), F32)],
        compiler_params=_params(("parallel", "parallel")),
        name="gla",
    )(a_part, a_part, a_part, a_part, s0, onorm_g.reshape(1, -1))


def _log_multiplicity(dist):
    count = np.zeros(dist.shape, np.float64)
    for window, dil in DILATIONS:
        count += (dist >= 0) & (dist <= window) & (dist % dil == 0)
    return np.where(count > 0, np.log(np.maximum(count, 1.0)), NEG_INF).astype(np.float32)


ATTN_TILE = 256


def _attn_kernel(q_ref, kt_ref, vt_ref, bias_ref, o_ref, k16_ref, v16_ref):
    s_len = q_ref.shape[1]
    n_tiles = s_len // ATTN_TILE
    k16_ref[...] = kt_ref[0].astype(BF16)
    v16_ref[...] = vt_ref[0].astype(BF16)
    lane = lax.broadcasted_iota(jnp.int32, (1, LANES), 1)
    heads = [lane < HD_B, lane >= HD_B]
    for i in range(n_tiles):
        rows = slice(i * ATTN_TILE, (i + 1) * ATTN_TILE)
        n_keys = (i + 1) * ATTN_TILE
        q = q_ref[0, rows, :]
        outs = []
        for head in heads:
            s = jnp.dot(jnp.where(head, q, jnp.zeros_like(q)), k16_ref[:, 0:n_keys],
                        preferred_element_type=F32)
            s = s + bias_ref[:, s_len - n_keys:s_len]
            p = jnp.exp(s - jnp.max(s, axis=-1, keepdims=True))
            den = jnp.sum(p, axis=-1, keepdims=True)
            pv = lax.dot_general(p.astype(BF16), v16_ref[:, 0:n_keys], NT_DIMS, preferred_element_type=F32)
            outs.append(pv / den)
        o_ref[0, rows, :] = jnp.where(heads[0], outs[0], outs[1]).astype(o_ref.dtype)


def _attn_prompt(q, k_t, v_t):
    bsz, s_len, bw = q.shape
    assert s_len % ATTN_TILE == 0 and bw % LANES == 0
    r = np.arange(ATTN_TILE)[:, None]
    c = np.arange(s_len)[None, :]
    bias = jnp.asarray(_log_multiplicity(r - (c - (s_len - ATTN_TILE))))
    tok = pl.BlockSpec((1, s_len, LANES), lambda b, h: (b, 0, h))
    feat = pl.BlockSpec((1, LANES, s_len), lambda b, h: (b, h, 0))
    return pl.pallas_call(
        _attn_kernel,
        grid=(bsz, bw // LANES),
        in_specs=[tok, feat, feat, _const_spec(bias.shape)],
        out_specs=tok,
        out_shape=jax.ShapeDtypeStruct((bsz, s_len, bw), BF16),
        scratch_shapes=[pltpu.VMEM((LANES, s_len), BF16), pltpu.VMEM((LANES, s_len), BF16)],
        compiler_params=_params(("parallel", "parallel")),
        name="attn",
    )(q, k_t, v_t, bias)


def _sattn_kernel(q_ref, kn_ref, vn_ref, kc_ref, vc_ref, ktail_ref, vtail_ref, bc_ref, bn_ref,
                  o_ref, ko_ref, vo_ref, *, t_len):
    bw, wb = kc_ref.shape[1], kc_ref.shape[2]
    nh = bw // HD_B
    assert nh == SUBLANES and wb % LANES == 0

    lane = lax.broadcasted_iota(jnp.int32, (1, LANES), 1)
    for src, tail, dst in ((kc_ref, ktail_ref, ko_ref), (vc_ref, vtail_ref, vo_ref)):
        cur = pltpu.roll(src[0, :, 0:LANES], LANES - t_len, 1)
        for c in range(wb // LANES):
            nxt_tile = src[0, :, (c + 1) * LANES:(c + 2) * LANES] if (c + 1) * LANES < wb else tail[0]
            nxt = pltpu.roll(nxt_tile, LANES - t_len, 1)
            dst[0, :, c * LANES:(c + 1) * LANES] = jnp.where(lane < LANES - t_len, cur, nxt)
            cur = nxt

    sub = lax.broadcasted_iota(jnp.int32, (SUBLANES, bw), 0)
    lane_head = lax.broadcasted_iota(jnp.int32, (SUBLANES, bw), 1) // HD_B
    own = sub == lane_head
    qv = q_ref[0]
    q_rows = jnp.concatenate(
        [jnp.where(own, jnp.broadcast_to(qv[t:t + 1, :], (SUBLANES, bw)), 0.0) for t in range(t_len)],
        axis=0).astype(BF16)

    s_c = jnp.dot(q_rows, kc_ref[0].astype(BF16), preferred_element_type=F32) + bc_ref[...]
    s_n = lax.dot_general(q_rows, kn_ref[0].astype(BF16), NT_DIMS, preferred_element_type=F32) + bn_ref[...]
    m = jnp.maximum(jnp.max(s_c, axis=-1, keepdims=True), jnp.max(s_n, axis=-1, keepdims=True))
    p_c = jnp.exp(s_c - m)
    p_n = jnp.exp(s_n - m)
    den = jnp.sum(p_c, axis=-1, keepdims=True) + jnp.sum(p_n, axis=-1, keepdims=True)
    o = lax.dot_general(p_c.astype(BF16), vc_ref[0].astype(BF16), NT_DIMS, preferred_element_type=F32)
    o = o + jnp.dot(p_n.astype(BF16), vn_ref[0].astype(BF16), preferred_element_type=F32)
    o = o / den
    rows = [jnp.sum(jnp.where(own, o[t * nh:(t + 1) * nh, :], 0.0), axis=0, keepdims=True)
            for t in range(t_len)]
    rows.append(jnp.zeros((SUBLANES - t_len, bw), F32))
    o_ref[0] = jnp.concatenate(rows, axis=0)


def _attn_sample(q, k_new, v_new, cache_kt, cache_vt, k_tail, v_tail, *, t_len):
    bsz, bw, wb = cache_kt.shape
    nh = bw // HD_B
    t = np.repeat(np.arange(t_len), nh)[:, None]
    bias_c = jnp.asarray(_log_multiplicity(wb + t - np.arange(wb)[None, :]))
    tn = np.arange(SUBLANES)[None, :]
    bias_n = jnp.asarray(np.where(tn < t_len, _log_multiplicity(t - tn), NEG_INF).astype(np.float32))
    small = pl.BlockSpec((1, SUBLANES, bw), lambda b: (b, 0, 0))
    big = pl.BlockSpec((1, bw, wb), lambda b: (b, 0, 0))
    tail = pl.BlockSpec((1, bw, LANES), lambda b: (b, 0, 0))
    return pl.pallas_call(
        functools.partial(_sattn_kernel, t_len=t_len),
        grid=(bsz,),
        in_specs=[small, small, small, big, big, tail, tail,
                  _const_spec(bias_c.shape), _const_spec(bias_n.shape)],
        out_specs=[small, big, big],
        out_shape=[jax.ShapeDtypeStruct((bsz, SUBLANES, bw), F32),
                   jax.ShapeDtypeStruct(cache_kt.shape, cache_kt.dtype),
                   jax.ShapeDtypeStruct(cache_vt.shape, cache_vt.dtype)],
        compiler_params=_params(("parallel",)),
        name="sattn",
    )(q, k_new, v_new, cache_kt, cache_vt, k_tail, v_tail, bias_c, bias_n)


def _post_kernel(x_ref, oa_ref, ob_ref, p_ref, cin_ref,
                 wo_ref, gf_ref, wg_ref, wu_ref, cw_ref, cb_ref, wd_ref, gp_ref, wpg_ref, wpp_ref, gl_ref,
                 y_ref, cout_ref, prev_ref, *, stride, ff_chunk):
    ts = x_ref.shape[1]
    d_ff = wg_ref.shape[1]
    aw = oa_ref.shape[2]
    n_prev = prev_ref.shape[0]

    @pl.when(pl.program_id(1) == 0)
    def _():
        prev_ref[...] = cin_ref[0]

    h = x_ref[0] + (jnp.dot(oa_ref[0], wo_ref[0:aw, :], preferred_element_type=F32)
                    + jnp.dot(ob_ref[0], wo_ref[aw:, :], preferred_element_type=F32))
    hn = _rmsnorm(h, gf_ref[...]).astype(BF16)

    sub = lax.broadcasted_iota(jnp.int32, (SUBLANES, 1), 0)
    for c0 in range(0, d_ff, ff_chunk):
        cs = slice(c0, min(c0 + ff_chunk, d_ff))
        u = jnp.dot(hn, wg_ref[:, cs], preferred_element_type=F32)
        prev = prev_ref[:, cs]
        if stride % SUBLANES == 0:
            u1 = jnp.concatenate([prev[stride:], u[:ts - stride]], axis=0)
            u2 = jnp.concatenate([prev, u[:ts - 2 * stride]], axis=0)
        else:
            assert stride == 1 and n_prev == SUBLANES
            r1, r2 = pltpu.roll(u, 1, 0), pltpu.roll(u, 2, 0)
            head1 = jnp.where(sub < 1, pltpu.roll(prev, 1, 0), r1[:SUBLANES])
            head2 = jnp.where(sub < 2, pltpu.roll(prev, 2, 0), r2[:SUBLANES])
            u1 = jnp.concatenate([head1, r1[SUBLANES:]], axis=0)
            u2 = jnp.concatenate([head2, r2[SUBLANES:]], axis=0)
        prev_ref[:, cs] = u[ts - n_prev:]
        conv = cb_ref[:, cs] + cw_ref[0:1, cs] * u2 + cw_ref[1:2, cs] * u1 + cw_ref[2:3, cs] * u
        up = jnp.dot(hn, wu_ref[:, cs], preferred_element_type=F32)
        act = (_silu(conv) * up).astype(BF16)
        down = jnp.dot(act, wd_ref[cs, :], preferred_element_type=F32)
        if c0 == 0:
            y_ref[0] = down
        else:
            y_ref[0] += down
    cout_ref[0] = prev_ref[...]

    h = h + y_ref[0]
    gate = _sigmoid(jnp.dot(_rmsnorm(h, gp_ref[...]).astype(BF16), wpg_ref[...], preferred_element_type=F32))
    h = h + gate * jnp.dot(p_ref[0].astype(BF16), wpp_ref[...], preferred_element_type=F32)
    y_ref[0] = _rmsnorm(h, gl_ref[...])


def _post(x, oa, ob, p, conv_in, w_o, g_ffn, w_gate, w_up, conv_w, conv_b, w_down, g_ple, w_pg, w_pp,
          g_fin, *, ts, stride):
    bsz, t_len, d = x.shape
    d_ff = w_gate.shape[1]
    n_prev = conv_in.shape[1]
    ff_chunk = -(-d_ff // (2 * MXU_WIDTH)) * MXU_WIDTH
    assert d_ff % MXU_WIDTH == 0 and t_len % ts == 0 and ts >= 2 * stride
    tok = lambda b, j: (b, j, 0)
    seq = lambda b, j: (b, 0, 0)
    row = lambda a: a.reshape(1, -1)
    weights = [w_o, row(g_ffn), w_gate, w_up, conv_w, row(conv_b), w_down, row(g_ple), w_pg, w_pp, row(g_fin)]
    return pl.pallas_call(
        functools.partial(_post_kernel, stride=stride, ff_chunk=ff_chunk),
        grid=(bsz, t_len // ts),
        in_specs=[pl.BlockSpec((1, ts, d), tok),
                  pl.BlockSpec((1, ts, oa.shape[2]), tok),
                  pl.BlockSpec((1, ts, ob.shape[2]), tok),
                  pl.BlockSpec((1, ts, p.shape[2]), tok),
                  pl.BlockSpec((1, n_prev, d_ff), seq)] + [_const_spec(w.shape) for w in weights],
        out_specs=[pl.BlockSpec((1, ts, d), tok), pl.BlockSpec((1, n_prev, d_ff), seq)],
        out_shape=[jax.ShapeDtypeStruct((bsz, t_len, d), F32),
                   jax.ShapeDtypeStruct((bsz, n_prev, d_ff), F32)],
        scratch_shapes=[pltpu.VMEM((n_prev, d_ff), F32)],
        compiler_params=_params(("parallel", "arbitrary")),
        name="post",
    )(x, oa, ob, p, conv_in, *weights)


PROMPT_TILE = 512
GLA_GROUP = 8
SAMPLE_SEQS = 8


def _feature_major(cache):
    bsz, wb, nh, hd = cache.shape
    return cache.transpose(0, 2, 3, 1).reshape(bsz, nh * hd, wb)


def _window_major(cache_t, nh):
    bsz, bw, wb = cache_t.shape
    return cache_t.reshape(bsz, nh, bw // nh, wb).transpose(0, 3, 1, 2)


def kernel(x_prompt, x_sample, state_hgrn, cache_win_k, cache_win_v, state_ffn_conv, p_prompt, p_sample,
           norm_attn_g, w_in, hgrn_lb_logits, hgrn_onorm_g, w_o, norm_ffn_g, w_gate, w_up, conv_w, conv_b,
           w_down, norm_ple_g, w_ple_gate, w_ple_proj, norm_final_g):
    depth = w_in.shape[0]
    assert depth == 1, "single-layer step"
    i = 0
    bsz, s_len, d = x_prompt.shape
    dbs, t_dec, _ = x_sample.shape
    nh_a, hd = state_hgrn.shape[2], state_hgrn.shape[3]
    wb, nh_b = cache_win_k.shape[2], cache_win_k.shape[3]
    bw = nh_b * HD_B
    aw = nh_a * hd
    d_ff = w_gate.shape[2]
    assert wb == min(WIN_MAX, PAST_LEN) and s_len <= WIN_MAX and t_dec <= SUBLANES
    n_conv = CONV_W - 1

    w_in16 = w_in[i].astype(BF16)
    w_aq16, w_kv16_t = w_in16[:, :4 * aw + bw], w_in16[:, 4 * aw + bw:].T
    w_gate16, w_up16, w_down16 = w_gate[i].astype(BF16), w_up[i].astype(BF16), w_down[i].astype(BF16)
    post_w = (w_o[i].astype(BF16), norm_ffn_g[i], w_gate16, w_up16, conv_w[i], conv_b[i], w_down16,
              norm_ple_g[i], w_ple_gate[i].astype(BF16), w_ple_proj[i].astype(BF16), norm_final_g)

    a_p, qb_p, kt_p, vt_p = _proj(x_prompt, norm_attn_g[i], w_aq16, w_kv16_t, hgrn_lb_logits,
                                  _rotary_tables(jnp.arange(s_len, dtype=F32)), layer=i, ts=PROMPT_TILE)
    oa_p, s_p = _gla(a_p, jnp.zeros((bsz, nh_a, hd, hd), F32), hgrn_onorm_g[i], chunk=min(HGRN_CHUNK, s_len),
                     group=GLA_GROUP, n_seq=1)
    ob_p = _attn_prompt(qb_p, kt_p, vt_p)
    y_p, conv_p = _post(x_prompt, oa_p, ob_p, p_prompt[i], jnp.zeros((bsz, SUBLANES, d_ff), F32), *post_w,
                        ts=PROMPT_TILE, stride=1)
    keep = min(WIN_MAX, s_len)

    n_tok = dbs * t_dec
    x_s = x_sample.transpose(1, 0, 2).reshape(1, n_tok, d)
    pos_s = jnp.repeat(float(PAST_LEN) + jnp.arange(t_dec, dtype=F32), dbs)
    a_s, qb_s, kt_s, vt_s = _proj(x_s, norm_attn_g[i], w_aq16, w_kv16_t, hgrn_lb_logits,
                                  _rotary_tables(pos_s), layer=i, ts=n_tok)

    def per_seq(a, fill=0.0):
        a = a.reshape(t_dec, dbs, a.shape[-1]).transpose(1, 0, 2).astype(F32)
        return jnp.pad(a, ((0, 0), (0, SUBLANES - t_dec), (0, 0)), constant_values=fill)

    def tail(a_t):
        a_t = a_t.reshape(a_t.shape[0], t_dec, dbs).transpose(2, 0, 1)
        return jnp.pad(a_t, ((0, 0), (0, 0), (0, LANES - t_dec)))

    a_s = a_s[0]
    a_s = jnp.concatenate([per_seq(a_s[:, :aw]), per_seq(a_s[:, aw:2 * aw], 1.0), per_seq(a_s[:, 2 * aw:])],
                          axis=-1)
    oa_s, s_s = _gla(a_s, state_hgrn[i].astype(F32), hgrn_onorm_g[i], chunk=SUBLANES, group=1,
                     n_seq=SAMPLE_SEQS)
    ob_s, kt_new, vt_new = _attn_sample(per_seq(qb_s[0]), per_seq(kt_s[0].T), per_seq(vt_s[0].T),
                                        _feature_major(cache_win_k[i]), _feature_major(cache_win_v[i]),
                                        tail(kt_s[0]), tail(vt_s[0]), t_len=t_dec)

    def tile_rows(a):
        return a[:, :t_dec].transpose(1, 0, 2).reshape(1, n_tok, a.shape[-1])

    conv_in_s = state_ffn_conv[i].transpose(1, 0, 2).reshape(1, n_conv * dbs, d_ff)
    p_s = p_sample[i].transpose(1, 0, 2).reshape(1, n_tok, -1)
    y_s, conv_s = _post(x_s, tile_rows(oa_s), tile_rows(ob_s).astype(BF16), p_s, conv_in_s, *post_w,
                        ts=n_tok, stride=dbs)

    return (y_p,
            y_s.reshape(t_dec, dbs, d).transpose(1, 0, 2),
            s_p[None],
            _window_major(kt_p, nh_b)[None, :, s_len - keep:],
            _window_major(vt_p, nh_b)[None, :, s_len - keep:],
            conv_p[None, :, SUBLANES - n_conv:],
            s_s[None],
            _window_major(kt_new, nh_b)[None],
            _window_major(vt_new, nh_b)[None],
            conv_s.reshape(n_conv, dbs, d_ff).transpose(1, 0, 2)[None])
```

```python
import functools

import numpy as np
import jax
import jax.numpy as jnp
from jax import lax
from jax.experimental import pallas as pl
from jax.experimental.pallas import tpu as pltpu

F32 = jnp.float32
BF16 = jnp.bfloat16

HGRN_HEAD_DIM = 128
HD_B = 64
ROT_DIM = HD_B // 4
ROPE_THETA = 500000.0
DILATIONS = ((128, 1), (512, 4), (2048, 16))
WIN_MAX = 2048
PAST_LEN = 16384
CONV_W = 3
EPS = 1e-6
NEG_INF = -1e30

SUBLANES = 8
LANES = 128
MXU_WIDTH = 256
VMEM_LIMIT_BYTES = 56 * 1024 * 1024

Q_SCALE = (HD_B ** -0.5) * float(np.log2(np.e))

NT_DIMS = (((1,), (1,)), ((), ()))
TN_DIMS = (((0,), (0,)), ((), ()))


def _rmsnorm(x, g):
    return x * lax.rsqrt(jnp.mean(x * x, axis=-1, keepdims=True) + EPS) * g


def _sigmoid(x):
    return 1.0 / (1.0 + jnp.exp(-x))


def _silu(x):
    return x * _sigmoid(x)


def _const_spec(shape):
    nd = len(shape)
    return pl.BlockSpec(shape, lambda *_: (0,) * nd, pipeline_mode=pl.Buffered(1))


def _params(semantics):
    return pltpu.CompilerParams(dimension_semantics=semantics, vmem_limit_bytes=VMEM_LIMIT_BYTES)


def _proj_kernel(x_ref, g_ref, w_ref, wkv_ref, lbl_ref, cos_ref, sin_ref, cost_ref, sint_ref,
                 a_ref, qb_ref, kt_ref, vt_ref, *, layer, a_width, b_width):
    x = x_ref[0]
    a = _rmsnorm(x, g_ref[...]).astype(BF16)

    def pj(j0, width):
        return jnp.dot(a, w_ref[:, j0:j0 + width], preferred_element_type=F32)

    lbl = lbl_ref[...]
    e = jnp.exp(lbl - jnp.max(lbl, axis=0, keepdims=True))
    lb = jnp.sum(e[:layer + 1], axis=0, keepdims=True) / jnp.sum(e, axis=0, keepdims=True)

    aw = a_width
    a_ref[0, :, 0:aw] = _silu(pj(0, aw)) * (HGRN_HEAD_DIM ** -0.5)
    a_ref[0, :, aw:2 * aw] = lb + (1.0 - lb) * _sigmoid(pj(aw, aw))
    a_ref[0, :, 2 * aw:3 * aw] = pj(2 * aw, aw)
    a_ref[0, :, 3 * aw:4 * aw] = _silu(pj(3 * aw, aw))

    half = ROT_DIM // 2
    lane = lax.broadcasted_iota(jnp.int32, (1, LANES), 1) % HD_B
    first_half = lane < half
    cos = cos_ref[...]
    sin = sin_ref[...]
    bw = b_width
    q = pj(4 * aw, bw)
    for c in range(bw // LANES):
        sl = slice(c * LANES, (c + 1) * LANES)
        t = q[:, sl]
        up = pltpu.roll(t, LANES - half, 1)
        dn = pltpu.roll(t, half, 1)
        rot = t * cos + jnp.where(first_half, up, dn) * sin
        qb_ref[0, :, sl] = (rot * Q_SCALE).astype(qb_ref.dtype)

    assert half == SUBLANES
    kt = lax.dot_general(wkv_ref[0:bw, :], a, NT_DIMS, preferred_element_type=F32)
    cos_t = cost_ref[...]
    sin_t = sint_ref[...]
    for h in range(bw // HD_B):
        r = h * HD_B
        x1 = kt[r:r + half]
        x2 = kt[r + half:r + ROT_DIM]
        kt_ref[0, r:r + half, :] = x1 * cos_t - x2 * sin_t
        kt_ref[0, r + half:r + ROT_DIM, :] = x2 * cos_t + x1 * sin_t
        kt_ref[0, r + ROT_DIM:r + HD_B, :] = kt[r + ROT_DIM:r + HD_B]
    vt_ref[0] = lax.dot_general(wkv_ref[bw:2 * bw, :], a, NT_DIMS, preferred_element_type=F32)


def _proj(x, g, w_aq, w_kv_t, lb_logits, tables, *, layer, ts):
    bsz, t_len, d = x.shape
    a_width = lb_logits.shape[1]
    b_width = w_kv_t.shape[0] // 2
    cos_n, sin_n, cos_t, sin_t = tables
    nt = t_len // ts
    tok = lambda j, b: (b, j, 0)
    feat = lambda j, b: (b, 0, j)
    return pl.pallas_call(
        functools.partial(_proj_kernel, layer=layer, a_width=a_width, b_width=b_width),
        grid=(nt, bsz),
        in_specs=[
            pl.BlockSpec((1, ts, d), tok),
            _const_spec((1, d)),
            _const_spec(w_aq.shape),
            _const_spec(w_kv_t.shape),
            _const_spec(lb_logits.shape),
            pl.BlockSpec((ts, LANES), lambda j, b: (j, 0)),
            pl.BlockSpec((ts, LANES), lambda j, b: (j, 0)),
            pl.BlockSpec((ROT_DIM // 2, ts), lambda j, b: (0, j)),
            pl.BlockSpec((ROT_DIM // 2, ts), lambda j, b: (0, j)),
        ],
        out_specs=[
            pl.BlockSpec((1, ts, 4 * a_width), tok),
            pl.BlockSpec((1, ts, b_width), tok),
            pl.BlockSpec((1, b_width, ts), feat),
            pl.BlockSpec((1, b_width, ts), feat),
        ],
        out_shape=[
            jax.ShapeDtypeStruct((bsz, t_len, 4 * a_width), F32),
            jax.ShapeDtypeStruct((bsz, t_len, b_width), BF16),
            jax.ShapeDtypeStruct((bsz, b_width, t_len), F32),
            jax.ShapeDtypeStruct((bsz, b_width, t_len), F32),
        ],
        compiler_params=_params(("parallel", "parallel")),
        name="proj",
    )(x, g.reshape(1, d), w_aq, w_kv_t, lb_logits, cos_n, sin_n, cos_t, sin_t)


def _rotary_tables(pos):
    half = ROT_DIM // 2
    inv_freq = jnp.power(ROPE_THETA, -jnp.arange(half, dtype=F32) * (2.0 / ROT_DIM))
    ang = pos[:, None] * inv_freq[None, :]
    cos, sin = jnp.cos(ang), jnp.sin(ang)
    t_len = pos.shape[0]
    ones = jnp.ones((t_len, HD_B - ROT_DIM), F32)
    cos_h = jnp.concatenate([cos, cos, ones], axis=1)
    sin_h = jnp.concatenate([-sin, sin, 0.0 * ones], axis=1)
    reps = LANES // HD_B
    return jnp.tile(cos_h, (1, reps)), jnp.tile(sin_h, (1, reps)), cos.T, sin.T


def _midpoint_rows(b_ref, base, half):
    if 2 * half >= SUBLANES:
        r = (base // (2 * half)) * (2 * half) + half - 1
        return jnp.broadcast_to(b_ref[r:r + 1, :], (SUBLANES, LANES))
    sub = lax.broadcasted_iota(jnp.int32, (SUBLANES, 1), 0)
    piece = None
    for blk in range(SUBLANES // (2 * half)):
        r = base + blk * 2 * half + half - 1
        row = jnp.broadcast_to(b_ref[r:r + 1, :], (SUBLANES, LANES))
        piece = row if piece is None else jnp.where(sub >= blk * 2 * half, row, piece)
    return piece


def _neg_abs(x):
    bits = lax.bitcast_convert_type(x, jnp.uint32) | jnp.uint32(0x80000000)
    return lax.bitcast_convert_type(bits, F32)


def _gla_kernel(q_ref, f_ref, v_ref, gs_ref, s0_ref, on_ref, o_ref, s_ref, st_ref, b_ref, *, chunk, group):
    n_seq, t_len = q_ref.shape[0], q_ref.shape[1]
    for sq in range(n_seq):
        st_ref[sq] = s0_ref[sq, 0].T

    row = lax.broadcasted_iota(jnp.int32, (chunk, chunk), 0)
    col = lax.broadcasted_iota(jnp.int32, (chunk, chunk), 1)
    differ = row ^ col
    halves = []
    h = 1
    while h < chunk:
        halves.append(h)
        h *= 2
    level = jnp.where(row < col, -1, 0)
    for n, half in enumerate(halves):
        level = jnp.where((row > col) & (differ >= half), n + 1, level)
    rows1 = lax.broadcasted_iota(jnp.int32, (chunk, 1), 0)
    onorm_g = on_ref[...]

    def one_chunk(sq, slot, r0, st):
        q = q_ref[sq, pl.ds(r0, chunk), :]
        fg = f_ref[sq, pl.ds(r0, chunk), :]
        v16 = v_ref[sq, pl.ds(r0, chunk), :].astype(BF16)
        g = jnp.log2(fg)
        k = 1.0 - fg

        b = g
        d = 1
        while d < chunk:
            b = b + jnp.where(rows1 >= d, pltpu.roll(b, d, 0), 0.0)
            d *= 2
        bs_ref = b_ref.at[slot]
        bs_ref[...] = b

        a_mat = jnp.where(level == 0,
                          lax.dot_general(q.astype(BF16), k.astype(BF16), NT_DIMS,
                                          preferred_element_type=F32), 0.0)
        zeros = jnp.zeros((SUBLANES, LANES), F32)
        for n, half in enumerate(halves):
            q_rows, k_rows = [], []
            for base in range(0, chunk, SUBLANES):
                rs = slice(base, base + SUBLANES)
                mid = _midpoint_rows(bs_ref, base, half)
                if half < SUBLANES:
                    e = jnp.exp2(_neg_abs(b[rs] - mid))
                    q_rows.append(q[rs] * e)
                    k_rows.append(k[rs] * e)
                elif (base // half) % 2:
                    q_rows.append(q[rs] * jnp.exp2(b[rs] - mid))
                    k_rows.append(zeros)
                else:
                    q_rows.append(zeros)
                    k_rows.append(k[rs] * jnp.exp2(mid - b[rs]))
            q_lvl = q_rows[0] if len(q_rows) == 1 else jnp.concatenate(q_rows, axis=0)
            k_lvl = k_rows[0] if len(k_rows) == 1 else jnp.concatenate(k_rows, axis=0)
            lvl = lax.dot_general(q_lvl.astype(BF16), k_lvl.astype(BF16), NT_DIMS,
                                  preferred_element_type=F32)
            a_mat = jnp.where(level == n + 1, lvl, a_mat)

        o = lax.dot_general((q * jnp.exp2(b)).astype(BF16), st.astype(BF16), NT_DIMS,
                            preferred_element_type=F32)
        o = o + jnp.dot(a_mat.astype(BF16), v16, preferred_element_type=F32)
        o = _rmsnorm(o, onorm_g) * gs_ref[sq, pl.ds(r0, chunk), :]
        o_ref[sq, pl.ds(r0, chunk), :] = o.astype(o_ref.dtype)

        b_last = bs_ref[chunk - 1:chunk, :]
        kd = (k * jnp.exp2(b_last - b)).astype(BF16)
        return st * jnp.exp2(b_last) + lax.dot_general(v16, kd, TN_DIMS, preferred_element_type=F32)

    def body(c, carry):
        for sq in range(n_seq):
            st = st_ref[sq]
            for u in range(group):
                r0 = pl.multiple_of((c * group + u) * chunk, chunk)
                st = one_chunk(sq, sq * group + u, r0, st)
            st_ref[sq] = st
        return carry

    lax.fori_loop(0, t_len // (chunk * group), body, 0)
    for sq in range(n_seq):
        s_ref[sq, 0] = st_ref[sq].T


def _gla(a_part, s0, onorm_g, *, chunk, group, n_seq):
    bsz, t_len, a4 = a_part.shape
    nh = s0.shape[1]
    hd = HGRN_HEAD_DIM
    assert bsz % n_seq == 0 and t_len % (chunk * group) == 0
    col = lambda part: pl.BlockSpec((n_seq, t_len, hd), lambda b, h, part=part: (b, 0, part * nh + h))
    return pl.pallas_call(
        functools.partial(_gla_kernel, chunk=chunk, group=group),
        grid=(bsz // n_seq, nh),
        in_specs=[col(0), col(1), col(2), col(3),
                  pl.BlockSpec((n_seq, 1, hd, hd), lambda b, h: (b, h, 0, 0)),
                  pl.BlockSpec((1, hd), lambda b, h: (0, h))],
        out_specs=[pl.BlockSpec((n_seq, t_len, hd), lambda b, h: (b, 0, h)),
                   pl.BlockSpec((n_seq, 1, hd, hd), lambda b, h: (b, h, 0, 0))],
        out_shape=[jax.ShapeDtypeStruct((bsz, t_len, a4 // 4), BF16),
                   jax.ShapeDtypeStruct(s0.shape, F32)],
        scratch_shapes=[pltpu.VMEM((n_seq, hd, hd), F32), pltpu.VMEM((n_seq * group, chunk, hd), F32)],
        compiler_params=_params(("parallel", "parallel")),
        name="gla",
    )(a_part, a_part, a_part, a_part, s0, onorm_g.reshape(1, -1))


def _log_multiplicity(dist):
    count = np.zeros(dist.shape, np.float64)
    for window, dil in DILATIONS:
        count += (dist >= 0) & (dist <= window) & (dist % dil == 0)
    return np.where(count > 0, np.log2(np.maximum(count, 1.0)), NEG_INF).astype(np.float32)


ATTN_TILE = 256


def _attn_kernel(q_ref, kt_ref, vt_ref, bias_ref, o_ref, k16_ref, v16_ref):
    s_len = q_ref.shape[1]
    n_tiles = s_len // ATTN_TILE
    k16_ref[...] = kt_ref[0].astype(BF16)
    feat = lax.broadcasted_iota(jnp.int32, (LANES, 1), 0)
    v = vt_ref[0]
    den_lane = [HD_B, 0]
    v16_ref[0] = jnp.where(feat < HD_B, v, jnp.where(feat == den_lane[0], 1.0, 0.0)).astype(BF16)
    v16_ref[1] = jnp.where(feat >= HD_B, v, jnp.where(feat == den_lane[1], 1.0, 0.0)).astype(BF16)
    lane = lax.broadcasted_iota(jnp.int32, (1, LANES), 1)
    heads = [lane < HD_B, lane >= HD_B]
    for i in range(n_tiles):
        rows = slice(i * ATTN_TILE, (i + 1) * ATTN_TILE)
        n_keys = (i + 1) * ATTN_TILE
        q = q_ref[0, rows, :]
        outs = []
        for hd, head in enumerate(heads):
            s = jnp.dot(jnp.where(head, q, jnp.zeros_like(q)), k16_ref[:, 0:n_keys],
                        preferred_element_type=F32)
            s = s + bias_ref[:, s_len - n_keys:s_len]
            p = jnp.exp2(s - jnp.max(s, axis=-1, keepdims=True))
            pv = lax.dot_general(p.astype(BF16), v16_ref[hd, :, 0:n_keys], NT_DIMS,
                                 preferred_element_type=F32)
            outs.append(pv / pv[:, den_lane[hd]:den_lane[hd] + 1])
        o_ref[0, rows, :] = jnp.where(heads[0], outs[0], outs[1]).astype(o_ref.dtype)


def _attn_prompt(q, k_t, v_t):
    bsz, s_len, bw = q.shape
    assert s_len % ATTN_TILE == 0 and bw % LANES == 0
    r = np.arange(ATTN_TILE)[:, None]
    c = np.arange(s_len)[None, :]
    bias = jnp.asarray(_log_multiplicity(r - (c - (s_len - ATTN_TILE))))
    tok = pl.BlockSpec((1, s_len, LANES), lambda b, h: (b, 0, h))
    feat = pl.BlockSpec((1, LANES, s_len), lambda b, h: (b, h, 0))
    return pl.pallas_call(
        _attn_kernel,
        grid=(bsz, bw // LANES),
        in_specs=[tok, feat, feat, _const_spec(bias.shape)],
        out_specs=tok,
        out_shape=jax.ShapeDtypeStruct((bsz, s_len, bw), BF16),
        scratch_shapes=[pltpu.VMEM((LANES, s_len), BF16), pltpu.VMEM((2, LANES, s_len), BF16)],
        compiler_params=_params(("parallel", "parallel")),
        name="attn",
    )(q, k_t, v_t, bias)


def _sattn_kernel(q_ref, kn_ref, vn_ref, kc_ref, vc_ref, ktail_ref, vtail_ref, bc_ref, bn_ref,
                  o_ref, ko_ref, vo_ref, *, t_len):
    bw, wb = kc_ref.shape[1], kc_ref.shape[2]
    nh = bw // HD_B
    assert nh == SUBLANES and wb % LANES == 0

    lane = lax.broadcasted_iota(jnp.int32, (1, LANES), 1)
    for src, tail, dst in ((kc_ref, ktail_ref, ko_ref), (vc_ref, vtail_ref, vo_ref)):
        cur = pltpu.roll(src[0, :, 0:LANES], LANES - t_len, 1)
        for c in range(wb // LANES):
            nxt_tile = src[0, :, (c + 1) * LANES:(c + 2) * LANES] if (c + 1) * LANES < wb else tail[0]
            nxt = pltpu.roll(nxt_tile, LANES - t_len, 1)
            dst[0, :, c * LANES:(c + 1) * LANES] = jnp.where(lane < LANES - t_len, cur, nxt)
            cur = nxt

    sub = lax.broadcasted_iota(jnp.int32, (SUBLANES, bw), 0)
    lane_head = lax.broadcasted_iota(jnp.int32, (SUBLANES, bw), 1) // HD_B
    own = sub == lane_head
    qv = q_ref[0]
    q_rows = jnp.concatenate(
        [jnp.where(own, jnp.broadcast_to(qv[t:t + 1, :], (SUBLANES, bw)), 0.0) for t in range(t_len)],
        axis=0).astype(BF16)

    s_c = jnp.dot(q_rows, kc_ref[0].astype(BF16), preferred_element_type=F32) + bc_ref[...]
    s_n = lax.dot_general(q_rows, kn_ref[0].astype(BF16), NT_DIMS, preferred_element_type=F32) + bn_ref[...]
    m = jnp.maximum(jnp.max(s_c, axis=-1, keepdims=True), jnp.max(s_n, axis=-1, keepdims=True))
    p_c = jnp.exp2(s_c - m)
    p_n = jnp.exp2(s_n - m)
    den = jnp.sum(p_c, axis=-1, keepdims=True) + jnp.sum(p_n, axis=-1, keepdims=True)
    o = lax.dot_general(p_c.astype(BF16), vc_ref[0].astype(BF16), NT_DIMS, preferred_element_type=F32)
    o = o + jnp.dot(p_n.astype(BF16), vn_ref[0].astype(BF16), preferred_element_type=F32)
    o = o / den
    rows = [jnp.sum(jnp.where(own, o[t * nh:(t + 1) * nh, :], 0.0), axis=0, keepdims=True)
            for t in range(t_len)]
    rows.append(jnp.zeros((SUBLANES - t_len, bw), F32))
    o_ref[0] = jnp.concatenate(rows, axis=0)


def _attn_sample(q, k_new, v_new, cache_kt, cache_vt, k_tail, v_tail, *, t_len):
    bsz, bw, wb = cache_kt.shape
    nh = bw // HD_B
    t = np.repeat(np.arange(t_len), nh)[:, None]
    bias_c = jnp.asarray(_log_multiplicity(wb + t - np.arange(wb)[None, :]))
    tn = np.arange(SUBLANES)[None, :]
    bias_n = jnp.asarray(np.where(tn < t_len, _log_multiplicity(t - tn), NEG_INF).astype(np.float32))
    small = pl.BlockSpec((1, SUBLANES, bw), lambda b: (b, 0, 0))
    big = pl.BlockSpec((1, bw, wb), lambda b: (b, 0, 0))
    tail = pl.BlockSpec((1, bw, LANES), lambda b: (b, 0, 0))
    return pl.pallas_call(
        functools.partial(_sattn_kernel, t_len=t_len),
        grid=(bsz,),
        in_specs=[small, small, small, big, big, tail, tail,
                  _const_spec(bias_c.shape), _const_spec(bias_n.shape)],
        out_specs=[small, big, big],
        out_shape=[jax.ShapeDtypeStruct((bsz, SUBLANES, bw), F32),
                   jax.ShapeDtypeStruct(cache_kt.shape, cache_kt.dtype),
                   jax.ShapeDtypeStruct(cache_vt.shape, cache_vt.dtype)],
        compiler_params=_params(("parallel",)),
        name="sattn",
    )(q, k_new, v_new, cache_kt, cache_vt, k_tail, v_tail, bias_c, bias_n)


def _post_kernel(x_ref, oa_ref, ob_ref, p_ref, cin_ref,
                 wo_ref, gf_ref, wg_ref, wu_ref, cw_ref, cb_ref, wd_ref, gp_ref, wpg_ref, wpp_ref, gl_ref,
                 y_ref, cout_ref, prev_ref, *, stride, ff_chunk):
    ts = x_ref.shape[1]
    d_ff = wg_ref.shape[1]
    aw = oa_ref.shape[2]
    n_prev = prev_ref.shape[0]

    @pl.when(pl.program_id(1) == 0)
    def _():
        prev_ref[...] = cin_ref[0]

    h = x_ref[0] + (jnp.dot(oa_ref[0], wo_ref[0:aw, :], preferred_element_type=F32)
                    + jnp.dot(ob_ref[0], wo_ref[aw:, :], preferred_element_type=F32))
    hn = _rmsnorm(h, gf_ref[...]).astype(BF16)

    sub = lax.broadcasted_iota(jnp.int32, (SUBLANES, 1), 0)
    for c0 in range(0, d_ff, ff_chunk):
        cs = slice(c0, min(c0 + ff_chunk, d_ff))
        u = jnp.dot(hn, wg_ref[:, cs], preferred_element_type=F32)
        prev = prev_ref[:, cs]
        if stride % SUBLANES == 0:
            u1 = jnp.concatenate([prev[stride:], u[:ts - stride]], axis=0)
            u2 = jnp.concatenate([prev, u[:ts - 2 * stride]], axis=0)
        else:
            assert stride == 1 and n_prev == SUBLANES
            r1, r2 = pltpu.roll(u, 1, 0), pltpu.roll(u, 2, 0)
            head1 = jnp.where(sub < 1, pltpu.roll(prev, 1, 0), r1[:SUBLANES])
            head2 = jnp.where(sub < 2, pltpu.roll(prev, 2, 0), r2[:SUBLANES])
            u1 = jnp.concatenate([head1, r1[SUBLANES:]], axis=0)
            u2 = jnp.concatenate([head2, r2[SUBLANES:]], axis=0)
        prev_ref[:, cs] = u[ts - n_prev:]
        conv = cb_ref[:, cs] + cw_ref[0:1, cs] * u2 + cw_ref[1:2, cs] * u1 + cw_ref[2:3, cs] * u
        up = jnp.dot(hn, wu_ref[:, cs], preferred_element_type=F32)
        act = (_silu(conv) * up).astype(BF16)
        down = jnp.dot(act, wd_ref[cs, :], preferred_element_type=F32)
        if c0 == 0:
            y_ref[0] = down
        else:
            y_ref[0] += down
    cout_ref[0] = prev_ref[...]

    h = h + y_ref[0]
    gate = _sigmoid(jnp.dot(_rmsnorm(h, gp_ref[...]).astype(BF16), wpg_ref[...], preferred_element_type=F32))
    h = h + gate * jnp.dot(p_ref[0].astype(BF16), wpp_ref[...], preferred_element_type=F32)
    y_ref[0] = _rmsnorm(h, gl_ref[...])


def _post(x, oa, ob, p, conv_in, w_o, g_ffn, w_gate, w_up, conv_w, conv_b, w_down, g_ple, w_pg, w_pp,
          g_fin, *, ts, stride):
    bsz, t_len, d = x.shape
    d_ff = w_gate.shape[1]
    n_prev = conv_in.shape[1]
    ff_chunk = -(-d_ff // (2 * MXU_WIDTH)) * MXU_WIDTH
    assert d_ff % MXU_WIDTH == 0 and t_len % ts == 0 and ts >= 2 * stride
    tok = lambda b, j: (b, j, 0)
    seq = lambda b, j: (b, 0, 0)
    row = lambda a: a.reshape(1, -1)
    weights = [w_o, row(g_ffn), w_gate, w_up, conv_w, row(conv_b), w_down, row(g_ple), w_pg, w_pp, row(g_fin)]
    return pl.pallas_call(
        functools.partial(_post_kernel, stride=stride, ff_chunk=ff_chunk),
        grid=(bsz, t_len // ts),
        in_specs=[pl.BlockSpec((1, ts, d), tok),
                  pl.BlockSpec((1, ts, oa.shape[2]), tok),
                  pl.BlockSpec((1, ts, ob.shape[2]), tok),
                  pl.BlockSpec((1, ts, p.shape[2]), tok),
                  pl.BlockSpec((1, n_prev, d_ff), seq)] + [_const_spec(w.shape) for w in weights],
        out_specs=[pl.BlockSpec((1, ts, d), tok), pl.BlockSpec((1, n_prev, d_ff), seq)],
        out_shape=[jax.ShapeDtypeStruct((bsz, t_len, d), F32),
                   jax.ShapeDtypeStruct((bsz, n_prev, d_ff), F32)],
        scratch_shapes=[pltpu.VMEM((n_prev, d_ff), F32)],
        compiler_params=_params(("parallel", "arbitrary")),
        name="post",
    )(x, oa, ob, p, conv_in, *weights)


PROMPT_TILE = 512
GLA_CHUNK = 128
GLA_GROUP = 8
SAMPLE_SEQS = 8


def _feature_major(cache):
    bsz, wb, nh, hd = cache.shape
    return cache.transpose(0, 2, 3, 1).reshape(bsz, nh * hd, wb)


def _window_major(cache_t, nh):
    bsz, bw, wb = cache_t.shape
    return cache_t.reshape(bsz, nh, bw // nh, wb).transpose(0, 3, 1, 2)


def kernel(x_prompt, x_sample, state_hgrn, cache_win_k, cache_win_v, state_ffn_conv, p_prompt, p_sample,
           norm_attn_g, w_in, hgrn_lb_logits, hgrn_onorm_g, w_o, norm_ffn_g, w_gate, w_up, conv_w, conv_b,
           w_down, norm_ple_g, w_ple_gate, w_ple_proj, norm_final_g):
    depth = w_in.shape[0]
    assert depth == 1, "single-layer step"
    i = 0
    bsz, s_len, d = x_prompt.shape
    dbs, t_dec, _ = x_sample.shape
    nh_a, hd = state_hgrn.shape[2], state_hgrn.shape[3]
    wb, nh_b = cache_win_k.shape[2], cache_win_k.shape[3]
    bw = nh_b * HD_B
    aw = nh_a * hd
    d_ff = w_gate.shape[2]
    assert wb == min(WIN_MAX, PAST_LEN) and s_len <= WIN_MAX and t_dec <= SUBLANES
    n_conv = CONV_W - 1

    w_in16 = w_in[i].astype(BF16)
    w_aq16, w_kv16_t = w_in16[:, :4 * aw + bw], w_in16[:, 4 * aw + bw:].T
    w_gate16, w_up16, w_down16 = w_gate[i].astype(BF16), w_up[i].astype(BF16), w_down[i].astype(BF16)
    post_w = (w_o[i].astype(BF16), norm_ffn_g[i], w_gate16, w_up16, conv_w[i], conv_b[i], w_down16,
              norm_ple_g[i], w_ple_gate[i].astype(BF16), w_ple_proj[i].astype(BF16), norm_final_g)

    a_p, qb_p, kt_p, vt_p = _proj(x_prompt, norm_attn_g[i], w_aq16, w_kv16_t, hgrn_lb_logits,
                                  _rotary_tables(jnp.arange(s_len, dtype=F32)), layer=i, ts=PROMPT_TILE)
    oa_p, s_p = _gla(a_p, jnp.zeros((bsz, nh_a, hd, hd), F32), hgrn_onorm_g[i], chunk=min(GLA_CHUNK, s_len),
                     group=GLA_GROUP, n_seq=1)
    ob_p = _attn_prompt(qb_p, kt_p, vt_p)
    y_p, conv_p = _post(x_prompt, oa_p, ob_p, p_prompt[i], jnp.zeros((bsz, SUBLANES, d_ff), F32), *post_w,
                        ts=PROMPT_TILE, stride=1)
    keep = min(WIN_MAX, s_len)

    n_tok = dbs * t_dec
    x_s = x_sample.transpose(1, 0, 2).reshape(1, n_tok, d)
    pos_s = jnp.repeat(float(PAST_LEN) + jnp.arange(t_dec, dtype=F32), dbs)
    a_s, qb_s, kt_s, vt_s = _proj(x_s, norm_attn_g[i], w_aq16, w_kv16_t, hgrn_lb_logits,
                                  _rotary_tables(pos_s), layer=i, ts=n_tok)

    def per_seq(a, fill=0.0):
        a = a.reshape(t_dec, dbs, a.shape[-1]).transpose(1, 0, 2).astype(F32)
        return jnp.pad(a, ((0, 0), (0, SUBLANES - t_dec), (0, 0)), constant_values=fill)

    def tail(a_t):
        a_t = a_t.reshape(a_t.shape[0], t_dec, dbs).transpose(2, 0, 1)
        return jnp.pad(a_t, ((0, 0), (0, 0), (0, LANES - t_dec)))

    a_s = a_s[0]
    a_s = jnp.concatenate([per_seq(a_s[:, :aw]), per_seq(a_s[:, aw:2 * aw], 1.0), per_seq(a_s[:, 2 * aw:])],
                          axis=-1)
    oa_s, s_s = _gla(a_s, state_hgrn[i].astype(F32), hgrn_onorm_g[i], chunk=SUBLANES, group=1,
                     n_seq=SAMPLE_SEQS)
    ob_s, kt_new, vt_new = _attn_sample(per_seq(qb_s[0]), per_seq(kt_s[0].T), per_seq(vt_s[0].T),
                                        _feature_major(cache_win_k[i]), _feature_major(cache_win_v[i]),
                                        tail(kt_s[0]), tail(vt_s[0]), t_len=t_dec)

    def tile_rows(a):
        return a[:, :t_dec].transpose(1, 0, 2).reshape(1, n_tok, a.shape[-1])

    conv_in_s = state_ffn_conv[i].transpose(1, 0, 2).reshape(1, n_conv * dbs, d_ff)
    p_s = p_sample[i].transpose(1, 0, 2).reshape(1, n_tok, -1)
    y_s, conv_s = _post(x_s, tile_rows(oa_s), tile_rows(ob_s).astype(BF16), p_s, conv_in_s, *post_w,
                        ts=n_tok, stride=dbs)

    return (y_p,
            y_s.reshape(t_dec, dbs, d).transpose(1, 0, 2),
            s_p[None],
            _window_major(kt_p, nh_b)[None, :, s_len - keep:],
            _window_major(vt_p, nh_b)[None, :, s_len - keep:],
            conv_p[None, :, SUBLANES - n_conv:],
            s_s[None],
            _window_major(kt_new, nh_b)[None],
            _window_major(vt_new, nh_b)[None],
            conv_s.reshape(n_conv, dbs, d_ff).transpose(1, 0, 2)[None])
```

```python
import functools

import numpy as np
import jax
import jax.numpy as jnp
from jax import lax
from jax.experimental import pallas as pl
from jax.experimental.pallas import tpu as pltpu

F32 = jnp.float32
BF16 = jnp.bfloat16

HGRN_HEAD_DIM = 128
HD_B = 64
ROT_DIM = HD_B // 4
ROPE_THETA = 500000.0
DILATIONS = ((128, 1), (512, 4), (2048, 16))
WIN_MAX = 2048
PAST_LEN = 16384
CONV_W = 3
EPS = 1e-6
NEG_INF = -1e30

SUBLANES = 8
LANES = 128
MXU_WIDTH = 256
VMEM_LIMIT_BYTES = 56 * 1024 * 1024

Q_SCALE = (HD_B ** -0.5) * float(np.log2(np.e))

NT_DIMS = (((1,), (1,)), ((), ()))
TN_DIMS = (((0,), (0,)), ((), ()))


def _rmsnorm(x, g):
    return x * lax.rsqrt(jnp.mean(x * x, axis=-1, keepdims=True) + EPS) * g


def _sigmoid(x):
    return 1.0 / (1.0 + jnp.exp(-x))


def _silu(x):
    return x * _sigmoid(x)


def _const_spec(shape):
    nd = len(shape)
    return pl.BlockSpec(shape, lambda *_: (0,) * nd, pipeline_mode=pl.Buffered(1))


def _params(semantics):
    return pltpu.CompilerParams(dimension_semantics=semantics, vmem_limit_bytes=VMEM_LIMIT_BYTES)


def _proj_kernel(x_ref, g_ref, w_ref, wkv_ref, lbl_ref, cos_ref, sin_ref, cost_ref, sint_ref,
                 a_ref, qb_ref, kt_ref, vt_ref, *, layer, a_width, b_width):
    x = x_ref[0]
    a = _rmsnorm(x, g_ref[...]).astype(BF16)

    def pj(j0, width):
        return jnp.dot(a, w_ref[:, j0:j0 + width], preferred_element_type=F32)

    lbl = lbl_ref[...]
    e = jnp.exp(lbl - jnp.max(lbl, axis=0, keepdims=True))
    lb = jnp.sum(e[:layer + 1], axis=0, keepdims=True) / jnp.sum(e, axis=0, keepdims=True)

    aw = a_width
    a_ref[0, :, 0:aw] = _silu(pj(0, aw)) * (HGRN_HEAD_DIM ** -0.5)
    a_ref[0, :, aw:2 * aw] = lb + (1.0 - lb) * _sigmoid(pj(aw, aw))
    a_ref[0, :, 2 * aw:3 * aw] = pj(2 * aw, aw)
    a_ref[0, :, 3 * aw:4 * aw] = _silu(pj(3 * aw, aw))

    half = ROT_DIM // 2
    lane = lax.broadcasted_iota(jnp.int32, (1, LANES), 1) % HD_B
    first_half = lane < half
    cos = cos_ref[...]
    sin = sin_ref[...]
    bw = b_width
    q = pj(4 * aw, bw)
    for c in range(bw // LANES):
        sl = slice(c * LANES, (c + 1) * LANES)
        t = q[:, sl]
        up = pltpu.roll(t, LANES - half, 1)
        dn = pltpu.roll(t, half, 1)
        rot = t * cos + jnp.where(first_half, up, dn) * sin
        qb_ref[0, :, sl] = (rot * Q_SCALE).astype(qb_ref.dtype)

    assert half == SUBLANES
    kt = lax.dot_general(wkv_ref[0:bw, :], a, NT_DIMS, preferred_element_type=F32)
    cos_t = cost_ref[...]
    sin_t = sint_ref[...]
    for h in range(bw // HD_B):
        r = h * HD_B
        x1 = kt[r:r + half]
        x2 = kt[r + half:r + ROT_DIM]
        kt_ref[0, r:r + half, :] = x1 * cos_t - x2 * sin_t
        kt_ref[0, r + half:r + ROT_DIM, :] = x2 * cos_t + x1 * sin_t
        kt_ref[0, r + ROT_DIM:r + HD_B, :] = kt[r + ROT_DIM:r + HD_B]
    vt_ref[0] = lax.dot_general(wkv_ref[bw:2 * bw, :], a, NT_DIMS, preferred_element_type=F32)


def _proj(x, g, w_aq, w_kv_t, lb_logits, tables, *, layer, ts):
    bsz, t_len, d = x.shape
    a_width = lb_logits.shape[1]
    b_width = w_kv_t.shape[0] // 2
    cos_n, sin_n, cos_t, sin_t = tables
    nt = t_len // ts
    tok = lambda j, b: (b, j, 0)
    feat = lambda j, b: (b, 0, j)
    return pl.pallas_call(
        functools.partial(_proj_kernel, layer=layer, a_width=a_width, b_width=b_width),
        grid=(nt, bsz),
        in_specs=[
            pl.BlockSpec((1, ts, d), tok),
            _const_spec((1, d)),
            _const_spec(w_aq.shape),
            _const_spec(w_kv_t.shape),
            _const_spec(lb_logits.shape),
            pl.BlockSpec((ts, LANES), lambda j, b: (j, 0)),
            pl.BlockSpec((ts, LANES), lambda j, b: (j, 0)),
            pl.BlockSpec((ROT_DIM // 2, ts), lambda j, b: (0, j)),
            pl.BlockSpec((ROT_DIM // 2, ts), lambda j, b: (0, j)),
        ],
        out_specs=[
            pl.BlockSpec((1, ts, 4 * a_width), tok),
            pl.BlockSpec((1, ts, b_width), tok),
            pl.BlockSpec((1, b_width, ts), feat),
            pl.BlockSpec((1, b_width, ts), feat),
        ],
        out_shape=[
            jax.ShapeDtypeStruct((bsz, t_len, 4 * a_width), F32),
            jax.ShapeDtypeStruct((bsz, t_len, b_width), BF16),
            jax.ShapeDtypeStruct((bsz, b_width, t_len), F32),
            jax.ShapeDtypeStruct((bsz, b_width, t_len), F32),
        ],
        compiler_params=_params(("parallel", "parallel")),
        name="proj",
    )(x, g.reshape(1, d), w_aq, w_kv_t, lb_logits, cos_n, sin_n, cos_t, sin_t)


def _rotary_tables(pos):
    half = ROT_DIM // 2
    inv_freq = jnp.power(ROPE_THETA, -jnp.arange(half, dtype=F32) * (2.0 / ROT_DIM))
    ang = pos[:, None] * inv_freq[None, :]
    cos, sin = jnp.cos(ang), jnp.sin(ang)
    t_len = pos.shape[0]
    ones = jnp.ones((t_len, HD_B - ROT_DIM), F32)
    cos_h = jnp.concatenate([cos, cos, ones], axis=1)
    sin_h = jnp.concatenate([-sin, sin, 0.0 * ones], axis=1)
    reps = LANES // HD_B
    return jnp.tile(cos_h, (1, reps)), jnp.tile(sin_h, (1, reps)), cos.T, sin.T


def _midpoint_rows(b_ref, base, half):
    if 2 * half >= SUBLANES:
        r = (base // (2 * half)) * (2 * half) + half - 1
        return jnp.broadcast_to(b_ref[r:r + 1, :], (SUBLANES, LANES))
    sub = lax.broadcasted_iota(jnp.int32, (SUBLANES, 1), 0)
    piece = None
    for blk in range(SUBLANES // (2 * half)):
        r = base + blk * 2 * half + half - 1
        row = jnp.broadcast_to(b_ref[r:r + 1, :], (SUBLANES, LANES))
        piece = row if piece is None else jnp.where(sub >= blk * 2 * half, row, piece)
    return piece


def _neg_abs(x):
    bits = lax.bitcast_convert_type(x, jnp.uint32) | jnp.uint32(0x80000000)
    return lax.bitcast_convert_type(bits, F32)


def _gla_kernel(q_ref, f_ref, v_ref, gs_ref, s0_ref, on_ref, o_ref, s_ref, st_ref, b_ref, *, chunk, group):
    n_seq, t_len = q_ref.shape[0], q_ref.shape[1]
    for sq in range(n_seq):
        st_ref[sq] = s0_ref[sq, 0].T

    row = lax.broadcasted_iota(jnp.int32, (chunk, chunk), 0)
    col = lax.broadcasted_iota(jnp.int32, (chunk, chunk), 1)
    differ = row ^ col
    halves = []
    h = 1
    while h < chunk:
        halves.append(h)
        h *= 2
    level = jnp.where(row < col, -1, 0)
    for n, half in enumerate(halves):
        level = jnp.where((row > col) & (differ >= half), n + 1, level)
    rows1 = lax.broadcasted_iota(jnp.int32, (chunk, 1), 0)
    onorm_g = on_ref[...]

    def one_chunk(sq, slot, r0, st):
        q = q_ref[sq, pl.ds(r0, chunk), :]
        fg = f_ref[sq, pl.ds(r0, chunk), :]
        v16 = v_ref[sq, pl.ds(r0, chunk), :].astype(BF16)
        g = jnp.log2(fg)
        k = 1.0 - fg

        b = g
        d = 1
        while d < chunk:
            b = b + jnp.where(rows1 >= d, pltpu.roll(b, d, 0), 0.0)
            d *= 2
        bs_ref = b_ref.at[slot]
        bs_ref[...] = b

        a_mat = jnp.where(level == 0,
                          lax.dot_general(q.astype(BF16), k.astype(BF16), NT_DIMS,
                                          preferred_element_type=F32), 0.0)
        zeros = jnp.zeros((SUBLANES, LANES), F32)
        for n, half in enumerate(halves):
            q_rows, k_rows = [], []
            for base in range(0, chunk, SUBLANES):
                rs = slice(base, base + SUBLANES)
                mid = _midpoint_rows(bs_ref, base, half)
                if half < SUBLANES:
                    e = jnp.exp2(_neg_abs(b[rs] - mid))
                    q_rows.append(q[rs] * e)
                    k_rows.append(k[rs] * e)
                elif (base // half) % 2:
                    q_rows.append(q[rs] * jnp.exp2(b[rs] - mid))
                    k_rows.append(zeros)
                else:
                    q_rows.append(zeros)
                    k_rows.append(k[rs] * jnp.exp2(mid - b[rs]))
            q_lvl = q_rows[0] if len(q_rows) == 1 else jnp.concatenate(q_rows, axis=0)
            k_lvl = k_rows[0] if len(k_rows) == 1 else jnp.concatenate(k_rows, axis=0)
            lvl = lax.dot_general(q_lvl.astype(BF16), k_lvl.astype(BF16), NT_DIMS,
                                  preferred_element_type=F32)
            a_mat = jnp.where(level == n + 1, lvl, a_mat)

        o = lax.dot_general((q * jnp.exp2(b)).astype(BF16), st.astype(BF16), NT_DIMS,
                            preferred_element_type=F32)
        o = o + jnp.dot(a_mat.astype(BF16), v16, preferred_element_type=F32)
        o = _rmsnorm(o, onorm_g) * gs_ref[sq, pl.ds(r0, chunk), :]
        o_ref[sq, pl.ds(r0, chunk), :] = o.astype(o_ref.dtype)

        b_last = bs_ref[chunk - 1:chunk, :]
        kd = (k * jnp.exp2(b_last - b)).astype(BF16)
        return st * jnp.exp2(b_last) + lax.dot_general(v16, kd, TN_DIMS, preferred_element_type=F32)

    def body(c, carry):
        for sq in range(n_seq):
            st = st_ref[sq]
            for u in range(group):
                r0 = pl.multiple_of((c * group + u) * chunk, chunk)
                st = one_chunk(sq, sq * group + u, r0, st)
            st_ref[sq] = st
        return carry

    lax.fori_loop(0, t_len // (chunk * group), body, 0)
    for sq in range(n_seq):
        s_ref[sq, 0] = st_ref[sq].T


def _gla(a_part, s0, onorm_g, *, chunk, group, n_seq):
    bsz, t_len, a4 = a_part.shape
    nh = s0.shape[1]
    hd = HGRN_HEAD_DIM
    assert bsz % n_seq == 0 and t_len % (chunk * group) == 0
    col = lambda part: pl.BlockSpec((n_seq, t_len, hd), lambda b, h, part=part: (b, 0, part * nh + h))
    return pl.pallas_call(
        functools.partial(_gla_kernel, chunk=chunk, group=group),
        grid=(bsz // n_seq, nh),
        in_specs=[col(0), col(1), col(2), col(3),
                  pl.BlockSpec((n_seq, 1, hd, hd), lambda b, h: (b, h, 0, 0)),
                  pl.BlockSpec((1, hd), lambda b, h: (0, h))],
        out_specs=[pl.BlockSpec((n_seq, t_len, hd), lambda b, h: (b, 0, h)),
                   pl.BlockSpec((n_seq, 1, hd, hd), lambda b, h: (b, h, 0, 0))],
        out_shape=[jax.ShapeDtypeStruct((bsz, t_len, a4 // 4), BF16),
                   jax.ShapeDtypeStruct(s0.shape, F32)],
        scratch_shapes=[pltpu.VMEM((n_seq, hd, hd), F32), pltpu.VMEM((n_seq * group, chunk, hd), F32)],
        compiler_params=_params(("parallel", "parallel")),
        name="gla",
    )(a_part, a_part, a_part, a_part, s0, onorm_g.reshape(1, -1))


def _log_multiplicity(dist):
    count = np.zeros(dist.shape, np.float64)
    for window, dil in DILATIONS:
        count += (dist >= 0) & (dist <= window) & (dist % dil == 0)
    return np.where(count > 0, np.log2(np.maximum(count, 1.0)), NEG_INF).astype(np.float32)


ATTN_TILE = 256


def _prompt_attention(q_ref, kt_ref, vt_ref, bias_ref, o_ref, k16_ref, v16_ref):
    s_len = q_ref.shape[1]
    n_tiles = s_len // ATTN_TILE
    k16_ref[...] = kt_ref[0].astype(BF16)
    feat = lax.broadcasted_iota(jnp.int32, (LANES, 1), 0)
    v = vt_ref[0]
    den_lane = [HD_B, 0]
    v16_ref[0] = jnp.where(feat < HD_B, v, jnp.where(feat == den_lane[0], 1.0, 0.0)).astype(BF16)
    v16_ref[1] = jnp.where(feat >= HD_B, v, jnp.where(feat == den_lane[1], 1.0, 0.0)).astype(BF16)
    lane = lax.broadcasted_iota(jnp.int32, (1, LANES), 1)
    heads = [lane < HD_B, lane >= HD_B]
    for i in range(n_tiles):
        rows = slice(i * ATTN_TILE, (i + 1) * ATTN_TILE)
        n_keys = (i + 1) * ATTN_TILE
        q = q_ref[0, rows, :]
        outs = []
        for hd, head in enumerate(heads):
            s = jnp.dot(jnp.where(head, q, jnp.zeros_like(q)), k16_ref[:, 0:n_keys],
                        preferred_element_type=F32)
            s = s + bias_ref[:, s_len - n_keys:s_len]
            p = jnp.exp2(s - jnp.max(s, axis=-1, keepdims=True))
            pv = lax.dot_general(p.astype(BF16), v16_ref[hd, :, 0:n_keys], NT_DIMS,
                                 preferred_element_type=F32)
            outs.append(pv / pv[:, den_lane[hd]:den_lane[hd] + 1])
        o_ref[0, rows, :] = jnp.where(heads[0], outs[0], outs[1]).astype(o_ref.dtype)


def _sample_attention(q_ref, kn_ref, vn_ref, kc_ref, vc_ref, ktail_ref, vtail_ref, bc_ref, bn_ref,
                      o_ref, ko_ref, vo_ref, *, t_len):
    fw, wb = kc_ref.shape[2], kc_ref.shape[3]
    nh = fw // HD_B
    assert nh <= SUBLANES and wb % LANES == 0

    lane = lax.broadcasted_iota(jnp.int32, (1, LANES), 1)
    for src, tail, dst in ((kc_ref, ktail_ref, ko_ref), (vc_ref, vtail_ref, vo_ref)):
        cur = pltpu.roll(src[0, 0, :, 0:LANES], LANES - t_len, 1)
        for c in range(wb // LANES):
            nxt_tile = src[0, 0, :, (c + 1) * LANES:(c + 2) * LANES] if (c + 1) * LANES < wb else tail[0, 0]
            nxt = pltpu.roll(nxt_tile, LANES - t_len, 1)
            dst[0, 0, :, c * LANES:(c + 1) * LANES] = jnp.where(lane < LANES - t_len, cur, nxt)
            cur = nxt

    sub = lax.broadcasted_iota(jnp.int32, (SUBLANES, fw), 0)
    lane_head = lax.broadcasted_iota(jnp.int32, (SUBLANES, fw), 1) // HD_B
    own = sub == lane_head
    qv = q_ref[0, 0]
    q_rows = jnp.concatenate(
        [jnp.where(own, jnp.broadcast_to(qv[t:t + 1, :], (SUBLANES, fw)), 0.0) for t in range(t_len)],
        axis=0).astype(BF16)

    s_c = jnp.dot(q_rows, kc_ref[0, 0].astype(BF16), preferred_element_type=F32) + bc_ref[...]
    s_n = lax.dot_general(q_rows, kn_ref[0, 0].astype(BF16), NT_DIMS, preferred_element_type=F32) + bn_ref[...]
    m = jnp.maximum(jnp.max(s_c, axis=-1, keepdims=True), jnp.max(s_n, axis=-1, keepdims=True))
    p_c = jnp.exp2(s_c - m)
    p_n = jnp.exp2(s_n - m)
    den = jnp.sum(p_c, axis=-1, keepdims=True) + jnp.sum(p_n, axis=-1, keepdims=True)
    o = lax.dot_general(p_c.astype(BF16), vc_ref[0, 0].astype(BF16), NT_DIMS, preferred_element_type=F32)
    o = o + jnp.dot(p_n.astype(BF16), vn_ref[0, 0].astype(BF16), preferred_element_type=F32)
    o = o / den
    rows = [jnp.sum(jnp.where(own, o[t * SUBLANES:(t + 1) * SUBLANES, :], 0.0), axis=0, keepdims=True)
            for t in range(t_len)]
    rows.append(jnp.zeros((SUBLANES - t_len, fw), F32))
    o_ref[0, 0] = jnp.concatenate(rows, axis=0)


def _attn_kernel(q_ref, kt_ref, vt_ref, bias_ref,
                 qs_ref, kn_ref, vn_ref, kc_ref, vc_ref, ktail_ref, vtail_ref, bc_ref, bn_ref,
                 o_ref, os_ref, ko_ref, vo_ref, k16_ref, v16_ref, *, t_len):
    _sample_attention(qs_ref, kn_ref, vn_ref, kc_ref, vc_ref, ktail_ref, vtail_ref, bc_ref, bn_ref,
                      os_ref, ko_ref, vo_ref, t_len=t_len)
    _prompt_attention(q_ref, kt_ref, vt_ref, bias_ref, o_ref, k16_ref, v16_ref)


def _attention(q, k_t, v_t, q_s, k_new, v_new, cache_kt, cache_vt, k_tail, v_tail, *, t_len):
    bsz, s_len, bw = q.shape
    n_seq, groups, fw, wb = cache_kt.shape
    n_pairs = bw // LANES
    assert s_len % ATTN_TILE == 0 and bw % LANES == 0 and bsz * n_pairs == n_seq * groups
    r = np.arange(ATTN_TILE)[:, None]
    c = np.arange(s_len)[None, :]
    bias = jnp.asarray(_log_multiplicity(r - (c - (s_len - ATTN_TILE))))
    t = np.repeat(np.arange(t_len), SUBLANES)[:, None]
    bias_c = jnp.asarray(_log_multiplicity(wb + t - np.arange(wb)[None, :]))
    tn = np.arange(SUBLANES)[None, :]
    bias_n = jnp.asarray(np.where(tn < t_len, _log_multiplicity(t - tn), NEG_INF).astype(np.float32))

    tok = pl.BlockSpec((1, s_len, LANES), lambda b, h: (b, 0, h))
    feat = pl.BlockSpec((1, LANES, s_len), lambda b, h: (b, h, 0))
    piece = lambda b, h: ((b * n_pairs + h) // groups, (b * n_pairs + h) % groups, 0, 0)
    small = pl.BlockSpec((1, 1, SUBLANES, fw), piece)
    big = pl.BlockSpec((1, 1, fw, wb), piece)
    tail = pl.BlockSpec((1, 1, fw, LANES), piece)
    return pl.pallas_call(
        functools.partial(_attn_kernel, t_len=t_len),
        grid=(bsz, n_pairs),
        in_specs=[tok, feat, feat, _const_spec(bias.shape),
                  small, small, small, big, big, tail, tail,
                  _const_spec(bias_c.shape), _const_spec(bias_n.shape)],
        out_specs=[tok, small, big, big],
        out_shape=[jax.ShapeDtypeStruct((bsz, s_len, bw), BF16),
                   jax.ShapeDtypeStruct(q_s.shape, F32),
                   jax.ShapeDtypeStruct(cache_kt.shape, cache_kt.dtype),
                   jax.ShapeDtypeStruct(cache_vt.shape, cache_vt.dtype)],
        scratch_shapes=[pltpu.VMEM((LANES, s_len), BF16), pltpu.VMEM((2, LANES, s_len), BF16)],
        compiler_params=_params(("parallel", "parallel")),
        name="attn",
    )(q, k_t, v_t, bias, q_s, k_new, v_new, cache_kt, cache_vt, k_tail, v_tail, bias_c, bias_n)


def _post_kernel(x_ref, oa_ref, ob_ref, p_ref, cin_ref,
                 wo_ref, gf_ref, wg_ref, wu_ref, cw_ref, cb_ref, wd_ref, gp_ref, wpg_ref, wpp_ref, gl_ref,
                 y_ref, cout_ref, prev_ref, *, stride, ff_chunk):
    ts = x_ref.shape[1]
    d_ff = wg_ref.shape[1]
    aw = oa_ref.shape[2]
    n_prev = prev_ref.shape[0]

    @pl.when(pl.program_id(1) == 0)
    def _():
        prev_ref[...] = cin_ref[0]

    h = x_ref[0] + (jnp.dot(oa_ref[0], wo_ref[0:aw, :], preferred_element_type=F32)
                    + jnp.dot(ob_ref[0], wo_ref[aw:, :], preferred_element_type=F32))
    hn = _rmsnorm(h, gf_ref[...]).astype(BF16)

    sub = lax.broadcasted_iota(jnp.int32, (SUBLANES, 1), 0)
    for c0 in range(0, d_ff, ff_chunk):
        cs = slice(c0, min(c0 + ff_chunk, d_ff))
        u = jnp.dot(hn, wg_ref[:, cs], preferred_element_type=F32)
        prev = prev_ref[:, cs]
        if stride % SUBLANES == 0:
            u1 = jnp.concatenate([prev[stride:], u[:ts - stride]], axis=0)
            u2 = jnp.concatenate([prev, u[:ts - 2 * stride]], axis=0)
        else:
            assert stride == 1 and n_prev == SUBLANES
            r1, r2 = pltpu.roll(u, 1, 0), pltpu.roll(u, 2, 0)
            head1 = jnp.where(sub < 1, pltpu.roll(prev, 1, 0), r1[:SUBLANES])
            head2 = jnp.where(sub < 2, pltpu.roll(prev, 2, 0), r2[:SUBLANES])
            u1 = jnp.concatenate([head1, r1[SUBLANES:]], axis=0)
            u2 = jnp.concatenate([head2, r2[SUBLANES:]], axis=0)
        prev_ref[:, cs] = u[ts - n_prev:]
        conv = cb_ref[:, cs] + cw_ref[0:1, cs] * u2 + cw_ref[1:2, cs] * u1 + cw_ref[2:3, cs] * u
        up = jnp.dot(hn, wu_ref[:, cs], preferred_element_type=F32)
        act = (_silu(conv) * up).astype(BF16)
        down = jnp.dot(act, wd_ref[cs, :], preferred_element_type=F32)
        if c0 == 0:
            y_ref[0] = down
        else:
            y_ref[0] += down
    cout_ref[0] = prev_ref[...]

    h = h + y_ref[0]
    gate = _sigmoid(jnp.dot(_rmsnorm(h, gp_ref[...]).astype(BF16), wpg_ref[...], preferred_element_type=F32))
    h = h + gate * jnp.dot(p_ref[0].astype(BF16), wpp_ref[...], preferred_element_type=F32)
    y_ref[0] = _rmsnorm(h, gl_ref[...])


def _post(x, oa, ob, p, conv_in, w_o, g_ffn, w_gate, w_up, conv_w, conv_b, w_down, g_ple, w_pg, w_pp,
          g_fin, *, ts, stride):
    bsz, t_len, d = x.shape
    d_ff = w_gate.shape[1]
    n_prev = conv_in.shape[1]
    ff_chunk = -(-d_ff // (2 * MXU_WIDTH)) * MXU_WIDTH
    assert d_ff % MXU_WIDTH == 0 and t_len % ts == 0 and ts >= 2 * stride
    tok = lambda b, j: (b, j, 0)
    seq = lambda b, j: (b, 0, 0)
    row = lambda a: a.reshape(1, -1)
    weights = [w_o, row(g_ffn), w_gate, w_up, conv_w, row(conv_b), w_down, row(g_ple), w_pg, w_pp, row(g_fin)]
    return pl.pallas_call(
        functools.partial(_post_kernel, stride=stride, ff_chunk=ff_chunk),
        grid=(bsz, t_len // ts),
        in_specs=[pl.BlockSpec((1, ts, d), tok),
                  pl.BlockSpec((1, ts, oa.shape[2]), tok),
                  pl.BlockSpec((1, ts, ob.shape[2]), tok),
                  pl.BlockSpec((1, ts, p.shape[2]), tok),
                  pl.BlockSpec((1, n_prev, d_ff), seq)] + [_const_spec(w.shape) for w in weights],
        out_specs=[pl.BlockSpec((1, ts, d), tok), pl.BlockSpec((1, n_prev, d_ff), seq)],
        out_shape=[jax.ShapeDtypeStruct((bsz, t_len, d), F32),
                   jax.ShapeDtypeStruct((bsz, n_prev, d_ff), F32)],
        scratch_shapes=[pltpu.VMEM((n_prev, d_ff), F32)],
        compiler_params=_params(("parallel", "arbitrary")),
        name="post",
    )(x, oa, ob, p, conv_in, *weights)


PROMPT_TILE = 512
GLA_CHUNK = 128
GLA_GROUP = 8
SAMPLE_SEQS = 8
SAMPLE_HEAD_GROUPS = 2


def _feature_major(cache):
    bsz, wb, nh, hd = cache.shape
    return cache.transpose(0, 2, 3, 1).reshape(bsz, nh * hd, wb)


def _window_major(cache_t, nh):
    bsz, bw, wb = cache_t.shape
    return cache_t.reshape(bsz, nh, bw // nh, wb).transpose(0, 3, 1, 2)


def kernel(x_prompt, x_sample, state_hgrn, cache_win_k, cache_win_v, state_ffn_conv, p_prompt, p_sample,
           norm_attn_g, w_in, hgrn_lb_logits, hgrn_onorm_g, w_o, norm_ffn_g, w_gate, w_up, conv_w, conv_b,
           w_down, norm_ple_g, w_ple_gate, w_ple_proj, norm_final_g):
    depth = w_in.shape[0]
    assert depth == 1, "single-layer step"
    i = 0
    bsz, s_len, d = x_prompt.shape
    dbs, t_dec, _ = x_sample.shape
    nh_a, hd = state_hgrn.shape[2], state_hgrn.shape[3]
    wb, nh_b = cache_win_k.shape[2], cache_win_k.shape[3]
    bw = nh_b * HD_B
    aw = nh_a * hd
    d_ff = w_gate.shape[2]
    assert wb == min(WIN_MAX, PAST_LEN) and s_len <= WIN_MAX and t_dec <= SUBLANES
    n_conv = CONV_W - 1

    w_in16 = w_in[i].astype(BF16)
    w_aq16, w_kv16_t = w_in16[:, :4 * aw + bw], w_in16[:, 4 * aw + bw:].T
    w_gate16, w_up16, w_down16 = w_gate[i].astype(BF16), w_up[i].astype(BF16), w_down[i].astype(BF16)
    post_w = (w_o[i].astype(BF16), norm_ffn_g[i], w_gate16, w_up16, conv_w[i], conv_b[i], w_down16,
              norm_ple_g[i], w_ple_gate[i].astype(BF16), w_ple_proj[i].astype(BF16), norm_final_g)

    a_p, qb_p, kt_p, vt_p = _proj(x_prompt, norm_attn_g[i], w_aq16, w_kv16_t, hgrn_lb_logits,
                                  _rotary_tables(jnp.arange(s_len, dtype=F32)), layer=i, ts=PROMPT_TILE)
    n_tok = dbs * t_dec
    x_s = x_sample.transpose(1, 0, 2).reshape(1, n_tok, d)
    pos_s = jnp.repeat(float(PAST_LEN) + jnp.arange(t_dec, dtype=F32), dbs)
    a_s, qb_s, kt_s, vt_s = _proj(x_s, norm_attn_g[i], w_aq16, w_kv16_t, hgrn_lb_logits,
                                  _rotary_tables(pos_s), layer=i, ts=n_tok)

    def per_seq(a, fill=0.0):
        a = a.reshape(t_dec, dbs, a.shape[-1]).transpose(1, 0, 2).astype(F32)
        return jnp.pad(a, ((0, 0), (0, SUBLANES - t_dec), (0, 0)), constant_values=fill)

    def tile_rows(a):
        return a[:, :t_dec].transpose(1, 0, 2).reshape(1, n_tok, a.shape[-1])

    oa_p, s_p = _gla(a_p, jnp.zeros((bsz, nh_a, hd, hd), F32), hgrn_onorm_g[i], chunk=min(GLA_CHUNK, s_len),
                     group=GLA_GROUP, n_seq=1)
    a_s = a_s[0]
    a_s = jnp.concatenate([per_seq(a_s[:, :aw]), per_seq(a_s[:, aw:2 * aw], 1.0), per_seq(a_s[:, 2 * aw:])],
                          axis=-1)
    oa_s, s_s = _gla(a_s, state_hgrn[i].astype(F32), hgrn_onorm_g[i], chunk=SUBLANES, group=1,
                     n_seq=SAMPLE_SEQS)

    grp = SAMPLE_HEAD_GROUPS
    fw = bw // grp

    def head_groups(a):
        return a.reshape(dbs, SUBLANES, grp, fw).transpose(0, 2, 1, 3)

    def tail(a_t):
        a_t = a_t.reshape(grp, fw, t_dec, dbs).transpose(3, 0, 1, 2)
        return jnp.pad(a_t, ((0, 0), (0, 0), (0, 0), (0, LANES - t_dec)))

    ob_p, ob_s, kt_new, vt_new = _attention(
        qb_p, kt_p, vt_p,
        head_groups(per_seq(qb_s[0])), head_groups(per_seq(kt_s[0].T)), head_groups(per_seq(vt_s[0].T)),
        _feature_major(cache_win_k[i]).reshape(dbs, grp, fw, wb),
        _feature_major(cache_win_v[i]).reshape(dbs, grp, fw, wb),
        tail(kt_s[0]), tail(vt_s[0]), t_len=t_dec)
    ob_s = ob_s.transpose(0, 2, 1, 3).reshape(dbs, SUBLANES, bw)
    kt_new, vt_new = kt_new.reshape(dbs, bw, wb), vt_new.reshape(dbs, bw, wb)

    y_p, conv_p = _post(x_prompt, oa_p, ob_p, p_prompt[i], jnp.zeros((bsz, SUBLANES, d_ff), F32), *post_w,
                        ts=PROMPT_TILE, stride=1)
    conv_in_s = state_ffn_conv[i].transpose(1, 0, 2).reshape(1, n_conv * dbs, d_ff)
    p_s = p_sample[i].transpose(1, 0, 2).reshape(1, n_tok, -1)
    y_s, conv_s = _post(x_s, tile_rows(oa_s), tile_rows(ob_s).astype(BF16), p_s, conv_in_s, *post_w,
                        ts=n_tok, stride=dbs)
    keep = min(WIN_MAX, s_len)

    return (y_p,
            y_s.reshape(t_dec, dbs, d).transpose(1, 0, 2),
            s_p[None],
            _window_major(kt_p, nh_b)[None, :, s_len - keep:],
            _window_major(vt_p, nh_b)[None, :, s_len - keep:],
            conv_p[None, :, SUBLANES - n_conv:],
            s_s[None],
            _window_major(kt_new, nh_b)[None],
            _window_major(vt_new, nh_b)[None],
            conv_s.reshape(n_conv, dbs, d_ff).transpose(1, 0, 2)[None])
```

```python
import functools

import numpy as np
import jax
import jax.numpy as jnp
from jax import lax
from jax.experimental import pallas as pl
from jax.experimental.pallas import tpu as pltpu

F32 = jnp.float32
BF16 = jnp.bfloat16

HGRN_HEAD_DIM = 128
HD_B = 64
ROT_DIM = HD_B // 4
ROPE_THETA = 500000.0
DILATIONS = ((128, 1), (512, 4), (2048, 16))
WIN_MAX = 2048
PAST_LEN = 16384
CONV_W = 3
EPS = 1e-6
NEG_INF = -1e30

SUBLANES = 8
LANES = 128
MXU_WIDTH = 256
VMEM_LIMIT_BYTES = 56 * 1024 * 1024

Q_SCALE = (HD_B ** -0.5) * float(np.log2(np.e))

NT_DIMS = (((1,), (1,)), ((), ()))
TN_DIMS = (((0,), (0,)), ((), ()))


def _rmsnorm(x, g):
    return x * lax.rsqrt(jnp.mean(x * x, axis=-1, keepdims=True) + EPS) * g


def _sigmoid(x):
    return 1.0 / (1.0 + jnp.exp(-x))


def _silu(x):
    return x * _sigmoid(x)


def _const_spec(shape):
    nd = len(shape)
    return pl.BlockSpec(shape, lambda *_: (0,) * nd, pipeline_mode=pl.Buffered(1))


def _params(semantics):
    return pltpu.CompilerParams(dimension_semantics=semantics, vmem_limit_bytes=VMEM_LIMIT_BYTES)


def _proj_kernel(x_ref, g_ref, w_ref, wkv_ref, lbl_ref, cos_ref, sin_ref, cost_ref, sint_ref,
                 a_ref, qb_ref, kt_ref, vt_ref, *, layer, a_width, b_width):
    x = x_ref[0]
    a = _rmsnorm(x, g_ref[...]).astype(BF16)

    def pj(j0, width):
        return jnp.dot(a, w_ref[:, j0:j0 + width], preferred_element_type=F32)

    lbl = lbl_ref[...]
    e = jnp.exp(lbl - jnp.max(lbl, axis=0, keepdims=True))
    lb = jnp.sum(e[:layer + 1], axis=0, keepdims=True) / jnp.sum(e, axis=0, keepdims=True)

    aw = a_width
    a_ref[0, :, 0:aw] = _silu(pj(0, aw)) * (HGRN_HEAD_DIM ** -0.5)
    a_ref[0, :, aw:2 * aw] = lb + (1.0 - lb) * _sigmoid(pj(aw, aw))
    a_ref[0, :, 2 * aw:3 * aw] = pj(2 * aw, aw)
    a_ref[0, :, 3 * aw:4 * aw] = _silu(pj(3 * aw, aw))

    half = ROT_DIM // 2
    lane = lax.broadcasted_iota(jnp.int32, (1, LANES), 1) % HD_B
    first_half = lane < half
    cos = cos_ref[...]
    sin = sin_ref[...]
    bw = b_width
    q = pj(4 * aw, bw)
    for c in range(bw // LANES):
        sl = slice(c * LANES, (c + 1) * LANES)
        t = q[:, sl]
        up = pltpu.roll(t, LANES - half, 1)
        dn = pltpu.roll(t, half, 1)
        rot = t * cos + jnp.where(first_half, up, dn) * sin
        qb_ref[0, :, sl] = (rot * Q_SCALE).astype(qb_ref.dtype)

    assert half == SUBLANES
    kt = lax.dot_general(wkv_ref[0:bw, :], a, NT_DIMS, preferred_element_type=F32)
    cos_t = cost_ref[...]
    sin_t = sint_ref[...]
    for h in range(bw // HD_B):
        r = h * HD_B
        x1 = kt[r:r + half]
        x2 = kt[r + half:r + ROT_DIM]
        kt_ref[0, r:r + half, :] = x1 * cos_t - x2 * sin_t
        kt_ref[0, r + half:r + ROT_DIM, :] = x2 * cos_t + x1 * sin_t
        kt_ref[0, r + ROT_DIM:r + HD_B, :] = kt[r + ROT_DIM:r + HD_B]
    vt_ref[0] = lax.dot_general(wkv_ref[bw:2 * bw, :], a, NT_DIMS, preferred_element_type=F32)


def _proj(x, g, w_aq, w_kv_t, lb_logits, tables, *, layer, ts):
    bsz, t_len, d = x.shape
    a_width = lb_logits.shape[1]
    b_width = w_kv_t.shape[0] // 2
    cos_n, sin_n, cos_t, sin_t = tables
    nt = t_len // ts
    tok = lambda j, b: (b, j, 0)
    feat = lambda j, b: (b, 0, j)
    return pl.pallas_call(
        functools.partial(_proj_kernel, layer=layer, a_width=a_width, b_width=b_width),
        grid=(nt, bsz),
        in_specs=[
            pl.BlockSpec((1, ts, d), tok),
            _const_spec((1, d)),
            _const_spec(w_aq.shape),
            _const_spec(w_kv_t.shape),
            _const_spec(lb_logits.shape),
            pl.BlockSpec((ts, LANES), lambda j, b: (j, 0)),
            pl.BlockSpec((ts, LANES), lambda j, b: (j, 0)),
            pl.BlockSpec((ROT_DIM // 2, ts), lambda j, b: (0, j)),
            pl.BlockSpec((ROT_DIM // 2, ts), lambda j, b: (0, j)),
        ],
        out_specs=[
            pl.BlockSpec((1, ts, 4 * a_width), tok),
            pl.BlockSpec((1, ts, b_width), tok),
            pl.BlockSpec((1, b_width, ts), feat),
            pl.BlockSpec((1, b_width, ts), feat),
        ],
        out_shape=[
            jax.ShapeDtypeStruct((bsz, t_len, 4 * a_width), F32),
            jax.ShapeDtypeStruct((bsz, t_len, b_width), BF16),
            jax.ShapeDtypeStruct((bsz, b_width, t_len), F32),
            jax.ShapeDtypeStruct((bsz, b_width, t_len), F32),
        ],
        compiler_params=_params(("parallel", "parallel")),
        name="proj",
    )(x, g.reshape(1, d), w_aq, w_kv_t, lb_logits, cos_n, sin_n, cos_t, sin_t)


def _rotary_tables(pos):
    half = ROT_DIM // 2
    inv_freq = jnp.power(ROPE_THETA, -jnp.arange(half, dtype=F32) * (2.0 / ROT_DIM))
    ang = pos[:, None] * inv_freq[None, :]
    cos, sin = jnp.cos(ang), jnp.sin(ang)
    t_len = pos.shape[0]
    ones = jnp.ones((t_len, HD_B - ROT_DIM), F32)
    cos_h = jnp.concatenate([cos, cos, ones], axis=1)
    sin_h = jnp.concatenate([-sin, sin, 0.0 * ones], axis=1)
    reps = LANES // HD_B
    return jnp.tile(cos_h, (1, reps)), jnp.tile(sin_h, (1, reps)), cos.T, sin.T


def _midpoint_rows(b_ref, base, half):
    if 2 * half >= SUBLANES:
        r = (base // (2 * half)) * (2 * half) + half - 1
        return jnp.broadcast_to(b_ref[r:r + 1, :], (SUBLANES, LANES))
    sub = lax.broadcasted_iota(jnp.int32, (SUBLANES, 1), 0)
    piece = None
    for blk in range(SUBLANES // (2 * half)):
        r = base + blk * 2 * half + half - 1
        row = jnp.broadcast_to(b_ref[r:r + 1, :], (SUBLANES, LANES))
        piece = row if piece is None else jnp.where(sub >= blk * 2 * half, row, piece)
    return piece


def _neg_abs(x):
    bits = lax.bitcast_convert_type(x, jnp.uint32) | jnp.uint32(0x80000000)
    return lax.bitcast_convert_type(bits, F32)


def _gla_levels(chunk):
    row = lax.broadcasted_iota(jnp.int32, (chunk, chunk), 0)
    col = lax.broadcasted_iota(jnp.int32, (chunk, chunk), 1)
    differ = row ^ col
    halves = []
    h = 1
    while h < chunk:
        halves.append(h)
        h *= 2
    level = jnp.where(row < col, -1, 0)
    for n, half in enumerate(halves):
        level = jnp.where((row > col) & (differ >= half), n + 1, level)
    return halves, level


def _gla_scores_steps(q, fg, bs_ref, halves, level, out):
    chunk = q.shape[0]
    g = jnp.log2(fg)
    k = 1.0 - fg

    rows1 = lax.broadcasted_iota(jnp.int32, (chunk, 1), 0)
    b = g
    d = 1
    while d < chunk:
        b = b + jnp.where(rows1 >= d, pltpu.roll(b, d, 0), 0.0)
        d *= 2
    bs_ref[...] = b

    a_mat = jnp.where(level == 0,
                      lax.dot_general(q.astype(BF16), k.astype(BF16), NT_DIMS,
                                      preferred_element_type=F32), 0.0)
    yield
    zeros = jnp.zeros((SUBLANES, LANES), F32)
    for n, half in enumerate(halves):
        q_rows, k_rows = [], []
        for base in range(0, chunk, SUBLANES):
            rs = slice(base, base + SUBLANES)
            mid = _midpoint_rows(bs_ref, base, half)
            if half < SUBLANES:
                e = jnp.exp2(_neg_abs(b[rs] - mid))
                q_rows.append(q[rs] * e)
                k_rows.append(k[rs] * e)
            elif (base // half) % 2:
                q_rows.append(q[rs] * jnp.exp2(b[rs] - mid))
                k_rows.append(zeros)
            else:
                q_rows.append(zeros)
                k_rows.append(k[rs] * jnp.exp2(mid - b[rs]))
        q_lvl = q_rows[0] if len(q_rows) == 1 else jnp.concatenate(q_rows, axis=0)
        k_lvl = k_rows[0] if len(k_rows) == 1 else jnp.concatenate(k_rows, axis=0)
        lvl = lax.dot_general(q_lvl.astype(BF16), k_lvl.astype(BF16), NT_DIMS,
                              preferred_element_type=F32)
        a_mat = jnp.where(level == n + 1, lvl, a_mat)
        yield
    out.extend((q, k, b, a_mat))


def _gla_output_steps(scores, v, gate, st, bs_ref, onorm_g, out):
    q, k, b, a_mat = scores
    chunk = q.shape[0]
    v16 = v.astype(BF16)
    o_inter = lax.dot_general((q * jnp.exp2(b)).astype(BF16), st.astype(BF16), NT_DIMS,
                              preferred_element_type=F32)
    yield
    o = o_inter + jnp.dot(a_mat.astype(BF16), v16, preferred_element_type=F32)
    yield
    b_last = bs_ref[chunk - 1:chunk, :]
    kd = (k * jnp.exp2(b_last - b)).astype(BF16)
    st = st * jnp.exp2(b_last) + lax.dot_general(v16, kd, TN_DIMS, preferred_element_type=F32)
    out.append(_rmsnorm(o, onorm_g) * gate)
    out.append(st)
    yield


def _gla_chunk(q, fg, v, gate, st, bs_ref, halves, level, onorm_g):
    scores, out = [], []
    for _ in _gla_scores_steps(q, fg, bs_ref, halves, level, scores):
        pass
    for _ in _gla_output_steps(scores, v, gate, st, bs_ref, onorm_g, out):
        pass
    return out


def _gla_kernel(q_ref, f_ref, v_ref, gs_ref, s0_ref, on_ref, o_ref, s_ref, st_ref, b_ref, *, chunk, group):
    n_seq, t_len = q_ref.shape[0], q_ref.shape[1]
    for sq in range(n_seq):
        st_ref[sq] = s0_ref[sq, 0].T
    halves, level = _gla_levels(chunk)
    onorm_g = on_ref[...]

    def body(c, carry):
        for sq in range(n_seq):
            st = st_ref[sq]
            for u in range(group):
                rows = pl.ds(pl.multiple_of((c * group + u) * chunk, chunk), chunk)
                o, st = _gla_chunk(q_ref[sq, rows, :], f_ref[sq, rows, :], v_ref[sq, rows, :],
                                   gs_ref[sq, rows, :], st, b_ref.at[sq * group + u], halves, level, onorm_g)
                o_ref[sq, rows, :] = o.astype(o_ref.dtype)
            st_ref[sq] = st
        return carry

    lax.fori_loop(0, t_len // (chunk * group), body, 0)
    for sq in range(n_seq):
        s_ref[sq, 0] = st_ref[sq].T


def _gla(a_part, s0, onorm_g, *, chunk, group, n_seq, n_batches=None):
    _, t_len, a4 = a_part.shape
    bsz = a_part.shape[0] if n_batches is None else n_batches
    nh = s0.shape[1]
    hd = HGRN_HEAD_DIM
    assert bsz % n_seq == 0 and t_len % (chunk * group) == 0
    col = lambda part: pl.BlockSpec((n_seq, t_len, hd), lambda b, h, part=part: (b, 0, part * nh + h))
    return pl.pallas_call(
        functools.partial(_gla_kernel, chunk=chunk, group=group),
        grid=(bsz // n_seq, nh),
        in_specs=[col(0), col(1), col(2), col(3),
                  pl.BlockSpec((n_seq, 1, hd, hd), lambda b, h: (b, h, 0, 0)),
                  pl.BlockSpec((1, hd), lambda b, h: (0, h))],
        out_specs=[pl.BlockSpec((n_seq, t_len, hd), lambda b, h: (b, 0, h)),
                   pl.BlockSpec((n_seq, 1, hd, hd), lambda b, h: (b, h, 0, 0))],
        out_shape=[jax.ShapeDtypeStruct((bsz, t_len, a4 // 4), BF16),
                   jax.ShapeDtypeStruct((bsz,) + s0.shape[1:], F32)],
        scratch_shapes=[pltpu.VMEM((n_seq, hd, hd), F32), pltpu.VMEM((n_seq * group, chunk, hd), F32)],
        compiler_params=_params(("parallel", "parallel")),
        name="gla",
    )(a_part, a_part, a_part, a_part, s0, onorm_g.reshape(1, -1))


def _log_multiplicity(dist):
    count = np.zeros(dist.shape, np.float64)
    for window, dil in DILATIONS:
        count += (dist >= 0) & (dist <= window) & (dist % dil == 0)
    return np.where(count > 0, np.log2(np.maximum(count, 1.0)), NEG_INF).astype(np.float32)


ATTN_TILE = 256


def _prompt_attention(q_ref, kt_ref, vt_ref, bias_ref, o_ref, k16_ref, v16_ref):
    s_len = q_ref.shape[1]
    n_tiles = s_len // ATTN_TILE
    k16_ref[...] = kt_ref[0].astype(BF16)
    feat = lax.broadcasted_iota(jnp.int32, (LANES, 1), 0)
    v = vt_ref[0]
    den_lane = [HD_B, 0]
    v16_ref[0] = jnp.where(feat < HD_B, v, jnp.where(feat == den_lane[0], 1.0, 0.0)).astype(BF16)
    v16_ref[1] = jnp.where(feat >= HD_B, v, jnp.where(feat == den_lane[1], 1.0, 0.0)).astype(BF16)
    lane = lax.broadcasted_iota(jnp.int32, (1, LANES), 1)
    heads = [lane < HD_B, lane >= HD_B]
    for i in range(n_tiles):
        rows = slice(i * ATTN_TILE, (i + 1) * ATTN_TILE)
        n_keys = (i + 1) * ATTN_TILE
        q = q_ref[0, rows, :]
        outs = []
        for hd, head in enumerate(heads):
            s = jnp.dot(jnp.where(head, q, jnp.zeros_like(q)), k16_ref[:, 0:n_keys],
                        preferred_element_type=F32)
            s = s + bias_ref[:, s_len - n_keys:s_len]
            p = jnp.exp2(s - jnp.max(s, axis=-1, keepdims=True))
            pv = lax.dot_general(p.astype(BF16), v16_ref[hd, :, 0:n_keys], NT_DIMS,
                                 preferred_element_type=F32)
            outs.append(pv / pv[:, den_lane[hd]:den_lane[hd] + 1])
        o_ref[0, rows, :] = jnp.where(heads[0], outs[0], outs[1]).astype(o_ref.dtype)


def _sample_attention(q_ref, kn_ref, vn_ref, kc_ref, vc_ref, ktail_ref, vtail_ref, bc_ref, bn_ref,
                      o_ref, ko_ref, vo_ref, *, t_len):
    fw, wb = kc_ref.shape[2], kc_ref.shape[3]
    nh = fw // HD_B
    assert nh <= SUBLANES and wb % LANES == 0

    lane = lax.broadcasted_iota(jnp.int32, (1, LANES), 1)
    for src, tail, dst in ((kc_ref, ktail_ref, ko_ref), (vc_ref, vtail_ref, vo_ref)):
        cur = pltpu.roll(src[0, 0, :, 0:LANES], LANES - t_len, 1)
        for c in range(wb // LANES):
            nxt_tile = src[0, 0, :, (c + 1) * LANES:(c + 2) * LANES] if (c + 1) * LANES < wb else tail[0, 0]
            nxt = pltpu.roll(nxt_tile, LANES - t_len, 1)
            dst[0, 0, :, c * LANES:(c + 1) * LANES] = jnp.where(lane < LANES - t_len, cur, nxt)
            cur = nxt

    sub = lax.broadcasted_iota(jnp.int32, (SUBLANES, fw), 0)
    lane_head = lax.broadcasted_iota(jnp.int32, (SUBLANES, fw), 1) // HD_B
    own = sub == lane_head
    qv = q_ref[0, 0]
    q_rows = jnp.concatenate(
        [jnp.where(own, jnp.broadcast_to(qv[t:t + 1, :], (SUBLANES, fw)), 0.0) for t in range(t_len)],
        axis=0).astype(BF16)

    s_c = jnp.dot(q_rows, kc_ref[0, 0].astype(BF16), preferred_element_type=F32) + bc_ref[...]
    s_n = lax.dot_general(q_rows, kn_ref[0, 0].astype(BF16), NT_DIMS, preferred_element_type=F32) + bn_ref[...]
    m = jnp.maximum(jnp.max(s_c, axis=-1, keepdims=True), jnp.max(s_n, axis=-1, keepdims=True))
    p_c = jnp.exp2(s_c - m)
    p_n = jnp.exp2(s_n - m)
    den = jnp.sum(p_c, axis=-1, keepdims=True) + jnp.sum(p_n, axis=-1, keepdims=True)
    o = lax.dot_general(p_c.astype(BF16), vc_ref[0, 0].astype(BF16), NT_DIMS, preferred_element_type=F32)
    o = o + jnp.dot(p_n.astype(BF16), vn_ref[0, 0].astype(BF16), preferred_element_type=F32)
    o = o / den
    rows = [jnp.sum(jnp.where(own, o[t * SUBLANES:(t + 1) * SUBLANES, :], 0.0), axis=0, keepdims=True)
            for t in range(t_len)]
    rows.append(jnp.zeros((SUBLANES - t_len, fw), F32))
    o_ref[0, 0] = jnp.concatenate(rows, axis=0)


def _attn_kernel(q_ref, kt_ref, vt_ref, bias_ref,
                 qs_ref, kn_ref, vn_ref, kc_ref, vc_ref, ktail_ref, vtail_ref, bc_ref, bn_ref,
                 o_ref, os_ref, ko_ref, vo_ref, k16_ref, v16_ref, *, t_len):
    _sample_attention(qs_ref, kn_ref, vn_ref, kc_ref, vc_ref, ktail_ref, vtail_ref, bc_ref, bn_ref,
                      os_ref, ko_ref, vo_ref, t_len=t_len)
    _prompt_attention(q_ref, kt_ref, vt_ref, bias_ref, o_ref, k16_ref, v16_ref)


def _attention(q, k_t, v_t, q_s, k_new, v_new, cache_kt, cache_vt, k_tail, v_tail, *, t_len):
    bsz, s_len, bw = q.shape
    n_seq, groups, fw, wb = cache_kt.shape
    n_pairs = bw // LANES
    assert s_len % ATTN_TILE == 0 and bw % LANES == 0 and bsz * n_pairs == n_seq * groups
    r = np.arange(ATTN_TILE)[:, None]
    c = np.arange(s_len)[None, :]
    bias = jnp.asarray(_log_multiplicity(r - (c - (s_len - ATTN_TILE))))
    t = np.repeat(np.arange(t_len), SUBLANES)[:, None]
    bias_c = jnp.asarray(_log_multiplicity(wb + t - np.arange(wb)[None, :]))
    tn = np.arange(SUBLANES)[None, :]
    bias_n = jnp.asarray(np.where(tn < t_len, _log_multiplicity(t - tn), NEG_INF).astype(np.float32))

    tok = pl.BlockSpec((1, s_len, LANES), lambda b, h: (b, 0, h))
    feat = pl.BlockSpec((1, LANES, s_len), lambda b, h: (b, h, 0))
    piece = lambda b, h: ((b * n_pairs + h) // groups, (b * n_pairs + h) % groups, 0, 0)
    small = pl.BlockSpec((1, 1, SUBLANES, fw), piece)
    big = pl.BlockSpec((1, 1, fw, wb), piece)
    tail = pl.BlockSpec((1, 1, fw, LANES), piece)
    return pl.pallas_call(
        functools.partial(_attn_kernel, t_len=t_len),
        grid=(bsz, n_pairs),
        in_specs=[tok, feat, feat, _const_spec(bias.shape),
                  small, small, small, big, big, tail, tail,
                  _const_spec(bias_c.shape), _const_spec(bias_n.shape)],
        out_specs=[tok, small, big, big],
        out_shape=[jax.ShapeDtypeStruct((bsz, s_len, bw), BF16),
                   jax.ShapeDtypeStruct(q_s.shape, F32),
                   jax.ShapeDtypeStruct(cache_kt.shape, cache_kt.dtype),
                   jax.ShapeDtypeStruct(cache_vt.shape, cache_vt.dtype)],
        scratch_shapes=[pltpu.VMEM((LANES, s_len), BF16), pltpu.VMEM((2, LANES, s_len), BF16)],
        compiler_params=_params(("parallel", "parallel")),
        name="attn",
    )(q, k_t, v_t, bias, q_s, k_new, v_new, cache_kt, cache_vt, k_tail, v_tail, bias_c, bias_n)


N_POST_WEIGHTS = 11


def _interleave(main, side, ratio):
    for _ in main:
        for _ in range(ratio):
            next(side, None)
    for _ in side:
        pass


def _post_kernel(*refs, stride, ff_piece, down_chunk, gla_chunk):
    x_ref, oa_ref, ob_ref, p_ref, cin_ref = refs[:5]
    (wo_ref, gf_ref, wg_ref, wu_ref, cw_ref, cb_ref, wd_ref, gp_ref, wpg_ref, wpp_ref,
     gl_ref) = refs[5:5 + N_POST_WEIGHTS]
    rest = refs[5 + N_POST_WEIGHTS:]
    if gla_chunk:
        (qa_ref, fa_ref, va_ref, ga_ref, s0_ref, on_ref, y_ref, cout_ref, sa_ref,
         prev_ref, act_ref, oa_scr, b_scr) = rest
    else:
        y_ref, cout_ref, prev_ref, act_ref = rest
    ts = x_ref.shape[1]
    d_ff = wg_ref.shape[1]
    aw = oa_ref.shape[2]
    n_prev = prev_ref.shape[0]
    b_idx, j = pl.program_id(0), pl.program_id(1)

    @pl.when(j == 0)
    def _():
        prev_ref[...] = cin_ref[0]

    if gla_chunk:
        n_heads = oa_scr.shape[1]

        @pl.when((b_idx == 0) & (j == 0))
        def _():
            oa_scr[0] = jnp.zeros(oa_scr.shape[1:], oa_scr.dtype)

        tile_rows = pl.ds(pl.multiple_of(j * ts, ts), ts)
        oa_here = jnp.concatenate([oa_scr[b_idx % 2, h, tile_rows, :] for h in range(n_heads)], axis=1)
        oa = jnp.where(b_idx == 0, oa_ref[0], oa_here)
    else:
        oa = oa_ref[0]

    def post_steps():
        mix = jnp.dot(oa, wo_ref[0:aw, :], preferred_element_type=F32)
        yield
        mix = mix + jnp.dot(ob_ref[0], wo_ref[aw:, :], preferred_element_type=F32)
        yield
        h = x_ref[0] + mix
        hn = _rmsnorm(h, gf_ref[...]).astype(BF16)

        sub = lax.broadcasted_iota(jnp.int32, (SUBLANES, 1), 0)
        for c0 in range(0, d_ff, ff_piece):
            cs = slice(c0, c0 + ff_piece)
            u = jnp.dot(hn, wg_ref[:, cs], preferred_element_type=F32)
            yield
            up = jnp.dot(hn, wu_ref[:, cs], preferred_element_type=F32)
            yield
            prev = prev_ref[:, cs]
            if stride % SUBLANES == 0:
                u1 = jnp.concatenate([prev[stride:], u[:ts - stride]], axis=0)
                u2 = jnp.concatenate([prev, u[:ts - 2 * stride]], axis=0)
            else:
                assert stride == 1 and n_prev == SUBLANES
                r1, r2 = pltpu.roll(u, 1, 0), pltpu.roll(u, 2, 0)
                head1 = jnp.where(sub < 1, pltpu.roll(prev, 1, 0), r1[:SUBLANES])
                head2 = jnp.where(sub < 2, pltpu.roll(prev, 2, 0), r2[:SUBLANES])
                u1 = jnp.concatenate([head1, r1[SUBLANES:]], axis=0)
                u2 = jnp.concatenate([head2, r2[SUBLANES:]], axis=0)
            prev_ref[:, cs] = u[ts - n_prev:]
            conv = cb_ref[:, cs] + cw_ref[0:1, cs] * u2 + cw_ref[1:2, cs] * u1 + cw_ref[2:3, cs] * u
            act_ref[:, cs] = (_silu(conv) * up).astype(BF16)
        cout_ref[0] = prev_ref[...]

        for k0 in range(0, d_ff, down_chunk):
            ks = slice(k0, min(k0 + down_chunk, d_ff))
            down = jnp.dot(act_ref[:, ks], wd_ref[ks, :], preferred_element_type=F32)
            if k0 == 0:
                y_ref[0] = down
            else:
                y_ref[0] += down
            yield

        h = h + y_ref[0]
        gate = jnp.dot(_rmsnorm(h, gp_ref[...]).astype(BF16), wpg_ref[...], preferred_element_type=F32)
        yield
        emb = jnp.dot(p_ref[0].astype(BF16), wpp_ref[...], preferred_element_type=F32)
        yield
        h = h + _sigmoid(gate) * emb
        y_ref[0] = _rmsnorm(h, gl_ref[...])

    def gla_steps():
        halves, level = _gla_levels(gla_chunk)
        onorm_g = on_ref[...]
        st = s0_ref[0, 0].T
        n_chunks = qa_ref.shape[1] // gla_chunk
        rows = lambda c: slice(c * gla_chunk, (c + 1) * gla_chunk)

        def scores_of(c, out):
            return _gla_scores_steps(qa_ref[0, rows(c), :], fa_ref[0, rows(c), :], b_scr.at[c], halves, level, out)

        scores = []
        yield from scores_of(0, scores)
        for c in range(n_chunks):
            ahead = []
            if c + 1 < n_chunks:
                yield from scores_of(c + 1, ahead)
            out = []
            yield from _gla_output_steps(scores, va_ref[0, rows(c), :], ga_ref[0, rows(c), :], st,
                                         b_scr.at[c], onorm_g, out)
            o, st = out
            oa_scr[(b_idx + 1) % 2, j, rows(c), :] = o.astype(oa_scr.dtype)
            scores = ahead
        sa_ref[0, 0] = st.T

    if gla_chunk:
        n_main = 4 + 2 * (d_ff // ff_piece) + -(-d_ff // down_chunk)
        n_side = (qa_ref.shape[1] // gla_chunk) * (gla_chunk.bit_length() + 3)
        _interleave(post_steps(), gla_steps(), -(-n_side // n_main))
    else:
        for _ in post_steps():
            pass


def _post(x, oa, ob, p, conv_in, w_o, g_ffn, w_gate, w_up, conv_w, conv_b, w_down, g_ple, w_pg, w_pp,
          g_fin, *, ts, stride, gla=None):
    bsz, t_len, d = x.shape
    d_ff = w_gate.shape[1]
    n_prev = conv_in.shape[1]
    nt = t_len // ts
    ff_piece, down_chunk = MXU_WIDTH, 3 * MXU_WIDTH
    assert d_ff % ff_piece == 0 and t_len % ts == 0 and ts >= 2 * stride
    tok = lambda b, j: (b, j, 0)
    seq = lambda b, j: (b, 0, 0)
    row = lambda a: a.reshape(1, -1)
    weights = [w_o, row(g_ffn), w_gate, w_up, conv_w, row(conv_b), w_down, row(g_ple), w_pg, w_pp, row(g_fin)]
    assert len(weights) == N_POST_WEIGHTS
    in_specs = [pl.BlockSpec((1, ts, d), tok),
                pl.BlockSpec((1, ts, oa.shape[2]), tok),
                pl.BlockSpec((1, ts, ob.shape[2]), tok),
                pl.BlockSpec((1, ts, p.shape[2]), tok),
                pl.BlockSpec((1, n_prev, d_ff), seq)] + [_const_spec(w.shape) for w in weights]
    out_specs = [pl.BlockSpec((1, ts, d), tok), pl.BlockSpec((1, n_prev, d_ff), seq)]
    out_shape = [jax.ShapeDtypeStruct((bsz, t_len, d), F32), jax.ShapeDtypeStruct((bsz, n_prev, d_ff), F32)]
    scratch = [pltpu.VMEM((n_prev, d_ff), F32), pltpu.VMEM((ts, d_ff), BF16)]
    operands = [x, oa, ob, p, conv_in, *weights]
    gla_chunk = None
    if gla is not None:
        a_part, s0, onorm_g, gla_chunk = gla
        nh, hd = s0.shape[1], HGRN_HEAD_DIM
        assert nt == nh and t_len % gla_chunk == 0 and oa.shape[0] == 1
        nxt = lambda b: jnp.minimum(b + 1, bsz - 1)
        in_specs[1] = pl.BlockSpec((1, ts, oa.shape[2]), lambda b, j: (0, jnp.where(b == 0, j, 0), 0))
        col = lambda part: pl.BlockSpec((1, t_len, hd), lambda b, j, part=part: (nxt(b), 0, part * nh + j))
        state = pl.BlockSpec((1, 1, hd, hd), lambda b, j: (nxt(b), j, 0, 0))
        in_specs += [col(0), col(1), col(2), col(3), state, pl.BlockSpec((1, hd), lambda b, j: (0, j))]
        out_specs.append(pl.BlockSpec((1, 1, hd, hd), lambda b, j: (b, j, 0, 0)))
        out_shape.append(jax.ShapeDtypeStruct(s0.shape, F32))
        scratch += [pltpu.VMEM((2, nh, t_len, hd), BF16), pltpu.VMEM((t_len // gla_chunk, gla_chunk, hd), F32)]
        operands += [a_part, a_part, a_part, a_part, s0, onorm_g.reshape(1, -1)]
    return pl.pallas_call(
        functools.partial(_post_kernel, stride=stride, ff_piece=ff_piece, down_chunk=down_chunk,
                          gla_chunk=gla_chunk),
        grid=(bsz, nt),
        in_specs=in_specs,
        out_specs=out_specs,
        out_shape=out_shape,
        scratch_shapes=scratch,
        compiler_params=_params(("arbitrary", "arbitrary")),
        name="post",
    )(*operands)


PROMPT_TILE = 512
GLA_CHUNK = 128
GLA_GROUP = 8
SAMPLE_SEQS = 8
SAMPLE_HEAD_GROUPS = 2


def _feature_major(cache):
    bsz, wb, nh, hd = cache.shape
    return cache.transpose(0, 2, 3, 1).reshape(bsz, nh * hd, wb)


def _window_major(cache_t, nh):
    bsz, bw, wb = cache_t.shape
    return cache_t.reshape(bsz, nh, bw // nh, wb).transpose(0, 3, 1, 2)


def kernel(x_prompt, x_sample, state_hgrn, cache_win_k, cache_win_v, state_ffn_conv, p_prompt, p_sample,
           norm_attn_g, w_in, hgrn_lb_logits, hgrn_onorm_g, w_o, norm_ffn_g, w_gate, w_up, conv_w, conv_b,
           w_down, norm_ple_g, w_ple_gate, w_ple_proj, norm_final_g):
    depth = w_in.shape[0]
    assert depth == 1, "single-layer step"
    i = 0
    bsz, s_len, d = x_prompt.shape
    dbs, t_dec, _ = x_sample.shape
    nh_a, hd = state_hgrn.shape[2], state_hgrn.shape[3]
    wb, nh_b = cache_win_k.shape[2], cache_win_k.shape[3]
    bw = nh_b * HD_B
    aw = nh_a * hd
    d_ff = w_gate.shape[2]
    assert wb == min(WIN_MAX, PAST_LEN) and s_len <= WIN_MAX and t_dec <= SUBLANES
    n_conv = CONV_W - 1

    w_in16 = w_in[i].astype(BF16)
    w_aq16, w_kv16_t = w_in16[:, :4 * aw + bw], w_in16[:, 4 * aw + bw:].T
    w_gate16, w_up16, w_down16 = w_gate[i].astype(BF16), w_up[i].astype(BF16), w_down[i].astype(BF16)
    post_w = (w_o[i].astype(BF16), norm_ffn_g[i], w_gate16, w_up16, conv_w[i], conv_b[i], w_down16,
              norm_ple_g[i], w_ple_gate[i].astype(BF16), w_ple_proj[i].astype(BF16), norm_final_g)

    a_p, qb_p, kt_p, vt_p = _proj(x_prompt, norm_attn_g[i], w_aq16, w_kv16_t, hgrn_lb_logits,
                                  _rotary_tables(jnp.arange(s_len, dtype=F32)), layer=i, ts=PROMPT_TILE)
    n_tok = dbs * t_dec
    x_s = x_sample.transpose(1, 0, 2).reshape(1, n_tok, d)
    pos_s = jnp.repeat(float(PAST_LEN) + jnp.arange(t_dec, dtype=F32), dbs)
    a_s, qb_s, kt_s, vt_s = _proj(x_s, norm_attn_g[i], w_aq16, w_kv16_t, hgrn_lb_logits,
                                  _rotary_tables(pos_s), layer=i, ts=n_tok)

    def per_seq(a, fill=0.0):
        a = a.reshape(t_dec, dbs, a.shape[-1]).transpose(1, 0, 2).astype(F32)
        return jnp.pad(a, ((0, 0), (0, SUBLANES - t_dec), (0, 0)), constant_values=fill)

    def tile_rows(a):
        return a[:, :t_dec].transpose(1, 0, 2).reshape(1, n_tok, a.shape[-1])

    s0_p = jnp.zeros((bsz, nh_a, hd, hd), F32)
    gla_chunk = min(GLA_CHUNK, s_len)
    oa_first, s_first = _gla(a_p, s0_p, hgrn_onorm_g[i], chunk=gla_chunk, group=GLA_GROUP, n_seq=1, n_batches=1)
    a_s = a_s[0]
    a_s = jnp.concatenate([per_seq(a_s[:, :aw]), per_seq(a_s[:, aw:2 * aw], 1.0), per_seq(a_s[:, 2 * aw:])],
                          axis=-1)
    oa_s, s_s = _gla(a_s, state_hgrn[i].astype(F32), hgrn_onorm_g[i], chunk=SUBLANES, group=1,
                     n_seq=SAMPLE_SEQS)

    grp = SAMPLE_HEAD_GROUPS
    fw = bw // grp

    def head_groups(a):
        return a.reshape(dbs, SUBLANES, grp, fw).transpose(0, 2, 1, 3)

    def tail(a_t):
        a_t = a_t.reshape(grp, fw, t_dec, dbs).transpose(3, 0, 1, 2)
        return jnp.pad(a_t, ((0, 0), (0, 0), (0, 0), (0, LANES - t_dec)))

    ob_p, ob_s, kt_new, vt_new = _attention(
        qb_p, kt_p, vt_p,
        head_groups(per_seq(qb_s[0])), head_groups(per_seq(kt_s[0].T)), head_groups(per_seq(vt_s[0].T)),
        _feature_major(cache_win_k[i]).reshape(dbs, grp, fw, wb),
        _feature_major(cache_win_v[i]).reshape(dbs, grp, fw, wb),
        tail(kt_s[0]), tail(vt_s[0]), t_len=t_dec)
    ob_s = ob_s.transpose(0, 2, 1, 3).reshape(dbs, SUBLANES, bw)
    kt_new, vt_new = kt_new.reshape(dbs, bw, wb), vt_new.reshape(dbs, bw, wb)

    y_p, conv_p, s_rest = _post(x_prompt, oa_first, ob_p, p_prompt[i], jnp.zeros((bsz, SUBLANES, d_ff), F32),
                                *post_w, ts=PROMPT_TILE, stride=1,
                                gla=(a_p, s0_p, hgrn_onorm_g[i], gla_chunk))
    s_p = jnp.concatenate([s_first, s_rest[:bsz - 1]], axis=0)
    conv_in_s = state_ffn_conv[i].transpose(1, 0, 2).reshape(1, n_conv * dbs, d_ff)
    p_s = p_sample[i].transpose(1, 0, 2).reshape(1, n_tok, -1)
    y_s, conv_s = _post(x_s, tile_rows(oa_s), tile_rows(ob_s).astype(BF16), p_s, conv_in_s, *post_w,
                        ts=n_tok, stride=dbs)
    keep = min(WIN_MAX, s_len)

    return (y_p,
            y_s.reshape(t_dec, dbs, d).transpose(1, 0, 2),
            s_p[None],
            _window_major(kt_p, nh_b)[None, :, s_len - keep:],
            _window_major(vt_p, nh_b)[None, :, s_len - keep:],
            conv_p[None, :, SUBLANES - n_conv:],
            s_s[None],
            _window_major(kt_new, nh_b)[None],
            _window_major(vt_new, nh_b)[None],
            conv_s.reshape(n_conv, dbs, d_ff).transpose(1, 0, 2)[None])
```

```python
import functools

import numpy as np
import jax
import jax.numpy as jnp
from jax import lax
from jax.experimental import pallas as pl
from jax.experimental.pallas import tpu as pltpu

F32 = jnp.float32
BF16 = jnp.bfloat16

HGRN_HEAD_DIM = 128
HD_B = 64
ROT_DIM = HD_B // 4
ROPE_THETA = 500000.0
DILATIONS = ((128, 1), (512, 4), (2048, 16))
WIN_MAX = 2048
PAST_LEN = 16384
CONV_W = 3
EPS = 1e-6
NEG_INF = -1e30

SUBLANES = 8
LANES = 128
MXU_WIDTH = 256
VMEM_LIMIT_BYTES = 56 * 1024 * 1024

Q_SCALE = (HD_B ** -0.5) * float(np.log2(np.e))

NT_DIMS = (((1,), (1,)), ((), ()))
TN_DIMS = (((0,), (0,)), ((), ()))


def _rmsnorm(x, g):
    return x * lax.rsqrt(jnp.mean(x * x, axis=-1, keepdims=True) + EPS) * g


def _sigmoid(x):
    return 1.0 / (1.0 + jnp.exp(-x))


def _silu(x):
    return x * _sigmoid(x)


def _const_spec(shape):
    nd = len(shape)
    return pl.BlockSpec(shape, lambda *_: (0,) * nd, pipeline_mode=pl.Buffered(1))


def _params(semantics):
    return pltpu.CompilerParams(dimension_semantics=semantics, vmem_limit_bytes=VMEM_LIMIT_BYTES)


def _proj_kernel(x_ref, g_ref, w_ref, wkv_ref, lbl_ref, cos_ref, sin_ref, cost_ref, sint_ref,
                 a_ref, qb_ref, kt_ref, vt_ref, *, layer, a_width, b_width):
    x = x_ref[0]
    a = _rmsnorm(x, g_ref[...]).astype(BF16)

    def pj(j0, width):
        return jnp.dot(a, w_ref[:, j0:j0 + width], preferred_element_type=F32)

    lbl = lbl_ref[...]
    e = jnp.exp(lbl - jnp.max(lbl, axis=0, keepdims=True))
    lb = jnp.sum(e[:layer + 1], axis=0, keepdims=True) / jnp.sum(e, axis=0, keepdims=True)

    aw = a_width
    a_ref[0, :, 0:aw] = _silu(pj(0, aw)) * (HGRN_HEAD_DIM ** -0.5)
    a_ref[0, :, aw:2 * aw] = lb + (1.0 - lb) * _sigmoid(pj(aw, aw))
    a_ref[0, :, 2 * aw:3 * aw] = pj(2 * aw, aw)
    a_ref[0, :, 3 * aw:4 * aw] = _silu(pj(3 * aw, aw))

    half = ROT_DIM // 2
    lane = lax.broadcasted_iota(jnp.int32, (1, LANES), 1) % HD_B
    first_half = lane < half
    cos = cos_ref[...]
    sin = sin_ref[...]
    bw = b_width
    q = pj(4 * aw, bw)
    for c in range(bw // LANES):
        sl = slice(c * LANES, (c + 1) * LANES)
        t = q[:, sl]
        up = pltpu.roll(t, LANES - half, 1)
        dn = pltpu.roll(t, half, 1)
        rot = t * cos + jnp.where(first_half, up, dn) * sin
        qb_ref[0, :, sl] = (rot * Q_SCALE).astype(qb_ref.dtype)

    assert half == SUBLANES
    kt = lax.dot_general(wkv_ref[0:bw, :], a, NT_DIMS, preferred_element_type=F32)
    cos_t = cost_ref[...]
    sin_t = sint_ref[...]
    for h in range(bw // HD_B):
        r = h * HD_B
        x1 = kt[r:r + half]
        x2 = kt[r + half:r + ROT_DIM]
        kt_ref[0, r:r + half, :] = x1 * cos_t - x2 * sin_t
        kt_ref[0, r + half:r + ROT_DIM, :] = x2 * cos_t + x1 * sin_t
        kt_ref[0, r + ROT_DIM:r + HD_B, :] = kt[r + ROT_DIM:r + HD_B]
    vt_ref[0] = lax.dot_general(wkv_ref[bw:2 * bw, :], a, NT_DIMS, preferred_element_type=F32)


def _proj(x, g, w_aq, w_kv_t, lb_logits, tables, *, layer, ts):
    bsz, t_len, d = x.shape
    a_width = lb_logits.shape[1]
    b_width = w_kv_t.shape[0] // 2
    cos_n, sin_n, cos_t, sin_t = tables
    nt = t_len // ts
    tok = lambda j, b: (b, j, 0)
    feat = lambda j, b: (b, 0, j)
    return pl.pallas_call(
        functools.partial(_proj_kernel, layer=layer, a_width=a_width, b_width=b_width),
        grid=(nt, bsz),
        in_specs=[
            pl.BlockSpec((1, ts, d), tok),
            _const_spec((1, d)),
            _const_spec(w_aq.shape),
            _const_spec(w_kv_t.shape),
            _const_spec(lb_logits.shape),
            pl.BlockSpec((ts, LANES), lambda j, b: (j, 0)),
            pl.BlockSpec((ts, LANES), lambda j, b: (j, 0)),
            pl.BlockSpec((ROT_DIM // 2, ts), lambda j, b: (0, j)),
            pl.BlockSpec((ROT_DIM // 2, ts), lambda j, b: (0, j)),
        ],
        out_specs=[
            pl.BlockSpec((1, ts, 4 * a_width), tok),
            pl.BlockSpec((1, ts, b_width), tok),
            pl.BlockSpec((1, b_width, ts), feat),
            pl.BlockSpec((1, b_width, ts), feat),
        ],
        out_shape=[
            jax.ShapeDtypeStruct((bsz, t_len, 4 * a_width), F32),
            jax.ShapeDtypeStruct((bsz, t_len, b_width), BF16),
            jax.ShapeDtypeStruct((bsz, b_width, t_len), F32),
            jax.ShapeDtypeStruct((bsz, b_width, t_len), F32),
        ],
        compiler_params=_params(("parallel", "parallel")),
        name="proj",
    )(x, g.reshape(1, d), w_aq, w_kv_t, lb_logits, cos_n, sin_n, cos_t, sin_t)


def _rotary_tables(pos):
    half = ROT_DIM // 2
    inv_freq = jnp.power(ROPE_THETA, -jnp.arange(half, dtype=F32) * (2.0 / ROT_DIM))
    ang = pos[:, None] * inv_freq[None, :]
    cos, sin = jnp.cos(ang), jnp.sin(ang)
    t_len = pos.shape[0]
    ones = jnp.ones((t_len, HD_B - ROT_DIM), F32)
    cos_h = jnp.concatenate([cos, cos, ones], axis=1)
    sin_h = jnp.concatenate([-sin, sin, 0.0 * ones], axis=1)
    reps = LANES // HD_B
    return jnp.tile(cos_h, (1, reps)), jnp.tile(sin_h, (1, reps)), cos.T, sin.T


def _midpoint_rows(b_ref, base, half):
    if 2 * half >= SUBLANES:
        r = (base // (2 * half)) * (2 * half) + half - 1
        return jnp.broadcast_to(b_ref[r:r + 1, :], (SUBLANES, LANES))
    sub = lax.broadcasted_iota(jnp.int32, (SUBLANES, 1), 0)
    piece = None
    for blk in range(SUBLANES // (2 * half)):
        r = base + blk * 2 * half + half - 1
        row = jnp.broadcast_to(b_ref[r:r + 1, :], (SUBLANES, LANES))
        piece = row if piece is None else jnp.where(sub >= blk * 2 * half, row, piece)
    return piece


def _neg_abs(x):
    bits = lax.bitcast_convert_type(x, jnp.uint32) | jnp.uint32(0x80000000)
    return lax.bitcast_convert_type(bits, F32)


def _gla_levels(chunk):
    row = lax.broadcasted_iota(jnp.int32, (chunk, chunk), 0)
    col = lax.broadcasted_iota(jnp.int32, (chunk, chunk), 1)
    differ = row ^ col
    halves = []
    h = 1
    while h < chunk:
        halves.append(h)
        h *= 2
    level = jnp.where(row < col, -1, 0)
    for n, half in enumerate(halves):
        level = jnp.where((row > col) & (differ >= half), n + 1, level)
    return halves, level


def _gla_scores_steps(q, fg, bs_ref, halves, level, out):
    chunk = q.shape[0]
    g = jnp.log2(fg)
    k = 1.0 - fg

    rows1 = lax.broadcasted_iota(jnp.int32, (chunk, 1), 0)
    b = g
    d = 1
    while d < chunk:
        b = b + jnp.where(rows1 >= d, pltpu.roll(b, d, 0), 0.0)
        d *= 2
    bs_ref[...] = b

    zeros = jnp.zeros((SUBLANES, LANES), F32)
    operands = [(q.astype(BF16), k.astype(BF16))]
    for n, half in enumerate(halves):
        q_rows, k_rows = [], []
        for base in range(0, chunk, SUBLANES):
            rs = slice(base, base + SUBLANES)
            mid = _midpoint_rows(bs_ref, base, half)
            if half < SUBLANES:
                e = jnp.exp2(_neg_abs(b[rs] - mid))
                q_rows.append(q[rs] * e)
                k_rows.append(k[rs] * e)
            elif (base // half) % 2:
                q_rows.append(q[rs] * jnp.exp2(b[rs] - mid))
                k_rows.append(zeros)
            else:
                q_rows.append(zeros)
                k_rows.append(k[rs] * jnp.exp2(mid - b[rs]))
        q_lvl = q_rows[0] if len(q_rows) == 1 else jnp.concatenate(q_rows, axis=0)
        k_lvl = k_rows[0] if len(k_rows) == 1 else jnp.concatenate(k_rows, axis=0)
        operands.append((q_lvl.astype(BF16), k_lvl.astype(BF16)))

    a_mat = None
    for n, (q_lvl, k_lvl) in enumerate(operands):
        lvl = lax.dot_general(q_lvl, k_lvl, NT_DIMS, preferred_element_type=F32)
        a_mat = jnp.where(level == n, lvl, 0.0 if a_mat is None else a_mat)
        yield
    out.extend((q, k, b, a_mat))


def _gla_output_steps(scores, v, gate, st, bs_ref, onorm_g, out):
    q, k, b, a_mat = scores
    chunk = q.shape[0]
    v16 = v.astype(BF16)
    o_inter = lax.dot_general((q * jnp.exp2(b)).astype(BF16), st.astype(BF16), NT_DIMS,
                              preferred_element_type=F32)
    yield
    o = o_inter + jnp.dot(a_mat.astype(BF16), v16, preferred_element_type=F32)
    yield
    b_last = bs_ref[chunk - 1:chunk, :]
    kd = (k * jnp.exp2(b_last - b)).astype(BF16)
    st = st * jnp.exp2(b_last) + lax.dot_general(v16, kd, TN_DIMS, preferred_element_type=F32)
    out.append(_rmsnorm(o, onorm_g) * gate)
    out.append(st)
    yield


def _gla_chunk(q, fg, v, gate, st, bs_ref, halves, level, onorm_g):
    scores, out = [], []
    for _ in _gla_scores_steps(q, fg, bs_ref, halves, level, scores):
        pass
    for _ in _gla_output_steps(scores, v, gate, st, bs_ref, onorm_g, out):
        pass
    return out


def _gla_kernel(q_ref, f_ref, v_ref, gs_ref, s0_ref, on_ref, o_ref, s_ref, st_ref, b_ref, *, chunk, group):
    n_seq, t_len = q_ref.shape[0], q_ref.shape[1]
    for sq in range(n_seq):
        st_ref[sq] = s0_ref[sq, 0].T
    halves, level = _gla_levels(chunk)
    onorm_g = on_ref[...]

    def body(c, carry):
        for sq in range(n_seq):
            st = st_ref[sq]
            for u in range(group):
                rows = pl.ds(pl.multiple_of((c * group + u) * chunk, chunk), chunk)
                o, st = _gla_chunk(q_ref[sq, rows, :], f_ref[sq, rows, :], v_ref[sq, rows, :],
                                   gs_ref[sq, rows, :], st, b_ref.at[sq * group + u], halves, level, onorm_g)
                o_ref[sq, rows, :] = o.astype(o_ref.dtype)
            st_ref[sq] = st
        return carry

    lax.fori_loop(0, t_len // (chunk * group), body, 0)
    for sq in range(n_seq):
        s_ref[sq, 0] = st_ref[sq].T


def _gla(a_part, s0, onorm_g, *, chunk, group, n_seq, n_batches=None):
    _, t_len, a4 = a_part.shape
    bsz = a_part.shape[0] if n_batches is None else n_batches
    nh = s0.shape[1]
    hd = HGRN_HEAD_DIM
    assert bsz % n_seq == 0 and t_len % (chunk * group) == 0
    col = lambda part: pl.BlockSpec((n_seq, t_len, hd), lambda b, h, part=part: (b, 0, part * nh + h))
    return pl.pallas_call(
        functools.partial(_gla_kernel, chunk=chunk, group=group),
        grid=(bsz // n_seq, nh),
        in_specs=[col(0), col(1), col(2), col(3),
                  pl.BlockSpec((n_seq, 1, hd, hd), lambda b, h: (b, h, 0, 0)),
                  pl.BlockSpec((1, hd), lambda b, h: (0, h))],
        out_specs=[pl.BlockSpec((n_seq, t_len, hd), lambda b, h: (b, 0, h)),
                   pl.BlockSpec((n_seq, 1, hd, hd), lambda b, h: (b, h, 0, 0))],
        out_shape=[jax.ShapeDtypeStruct((bsz, t_len, a4 // 4), BF16),
                   jax.ShapeDtypeStruct((bsz,) + s0.shape[1:], F32)],
        scratch_shapes=[pltpu.VMEM((n_seq, hd, hd), F32), pltpu.VMEM((n_seq * group, chunk, hd), F32)],
        compiler_params=_params(("parallel", "parallel")),
        name="gla",
    )(a_part, a_part, a_part, a_part, s0, onorm_g.reshape(1, -1))


def _log_multiplicity(dist):
    count = np.zeros(dist.shape, np.float64)
    for window, dil in DILATIONS:
        count += (dist >= 0) & (dist <= window) & (dist % dil == 0)
    return np.where(count > 0, np.log2(np.maximum(count, 1.0)), NEG_INF).astype(np.float32)


ATTN_TILE = 256
ATTN_LOOKAHEAD = 1


def _prompt_attention(q_ref, kt_ref, vt_ref, bias_ref, o_ref, k16_ref, v16_ref):
    s_len = q_ref.shape[1]
    n_tiles = s_len // ATTN_TILE
    k16_ref[...] = kt_ref[0].astype(BF16)
    feat = lax.broadcasted_iota(jnp.int32, (LANES, 1), 0)
    v = vt_ref[0]
    den_lane = [HD_B, 0]
    v16_ref[0] = jnp.where(feat < HD_B, v, jnp.where(feat == den_lane[0], 1.0, 0.0)).astype(BF16)
    v16_ref[1] = jnp.where(feat >= HD_B, v, jnp.where(feat == den_lane[1], 1.0, 0.0)).astype(BF16)
    lane = lax.broadcasted_iota(jnp.int32, (1, LANES), 1)
    heads = [lane < HD_B, lane >= HD_B]
    def probabilities(i, hd):
        n_keys = (i + 1) * ATTN_TILE
        q = q_ref[0, i * ATTN_TILE:(i + 1) * ATTN_TILE, :]
        s = jnp.dot(jnp.where(heads[hd], q, jnp.zeros_like(q)), k16_ref[:, 0:n_keys],
                    preferred_element_type=F32)
        s = s + bias_ref[:, s_len - n_keys:s_len]
        return jnp.exp2(s - jnp.max(s, axis=-1, keepdims=True)).astype(BF16)

    blocks = [(i, hd) for i in range(n_tiles) for hd in range(len(heads))]
    outs = {}
    ahead = [probabilities(*blk) for blk in blocks[:ATTN_LOOKAHEAD]]
    for n, (i, hd) in enumerate(blocks):
        p = ahead.pop(0)
        if n + ATTN_LOOKAHEAD < len(blocks):
            ahead.append(probabilities(*blocks[n + ATTN_LOOKAHEAD]))
        pv = lax.dot_general(p, v16_ref[hd, :, 0:(i + 1) * ATTN_TILE], NT_DIMS, preferred_element_type=F32)
        outs[hd] = pv / pv[:, den_lane[hd]:den_lane[hd] + 1]
        if hd == len(heads) - 1:
            o_ref[0, i * ATTN_TILE:(i + 1) * ATTN_TILE, :] = jnp.where(heads[0], outs[0], outs[1]).astype(o_ref.dtype)


def _sample_attention(seq, q_ref, kn_ref, vn_ref, kc_ref, vc_ref, ktail_ref, vtail_ref, bc_ref, bn_ref,
                      o_ref, ko_ref, vo_ref, *, t_len):
    fw, wb = kc_ref.shape[2], kc_ref.shape[3]
    nh = fw // HD_B
    assert nh <= SUBLANES and wb % LANES == 0

    lane = lax.broadcasted_iota(jnp.int32, (1, LANES), 1)
    for src, tail, dst in ((kc_ref, ktail_ref, ko_ref), (vc_ref, vtail_ref, vo_ref)):
        cur = pltpu.roll(src[0, 0, :, 0:LANES], LANES - t_len, 1)
        for c in range(wb // LANES):
            if (c + 1) * LANES < wb:
                nxt = pltpu.roll(src[0, 0, :, (c + 1) * LANES:(c + 2) * LANES], LANES - t_len, 1)
            else:
                nxt = pltpu.roll(tail[0], (LANES - t_len - t_len * seq) % LANES, 1)
            dst[0, 0, :, c * LANES:(c + 1) * LANES] = jnp.where(lane < LANES - t_len, cur, nxt)
            cur = nxt

    sub = lax.broadcasted_iota(jnp.int32, (SUBLANES, fw), 0)
    lane_head = lax.broadcasted_iota(jnp.int32, (SUBLANES, fw), 1) // HD_B
    own = sub == lane_head
    qv = q_ref[0, 0]
    q_rows = jnp.concatenate(
        [jnp.where(own, jnp.broadcast_to(qv[t:t + 1, :], (SUBLANES, fw)), 0.0) for t in range(t_len)],
        axis=0).astype(BF16)

    s_c = jnp.dot(q_rows, kc_ref[0, 0].astype(BF16), preferred_element_type=F32) + bc_ref[...]
    s_n = lax.dot_general(q_rows, kn_ref[0, 0].astype(BF16), NT_DIMS, preferred_element_type=F32) + bn_ref[...]
    m = jnp.maximum(jnp.max(s_c, axis=-1, keepdims=True), jnp.max(s_n, axis=-1, keepdims=True))
    p_c = jnp.exp2(s_c - m)
    p_n = jnp.exp2(s_n - m)
    den = jnp.sum(p_c, axis=-1, keepdims=True) + jnp.sum(p_n, axis=-1, keepdims=True)
    o = lax.dot_general(p_c.astype(BF16), vc_ref[0, 0].astype(BF16), NT_DIMS, preferred_element_type=F32)
    o = o + jnp.dot(p_n.astype(BF16), vn_ref[0, 0].astype(BF16), preferred_element_type=F32)
    o = o / den
    rows = [jnp.sum(jnp.where(own, o[t * SUBLANES:(t + 1) * SUBLANES, :], 0.0), axis=0, keepdims=True)
            for t in range(t_len)]
    rows.append(jnp.zeros((SUBLANES - t_len, fw), F32))
    o_ref[0, 0] = jnp.concatenate(rows, axis=0)


def _attn_kernel(q_ref, kt_ref, vt_ref, bias_ref,
                 qs_ref, kn_ref, vn_ref, kc_ref, vc_ref, ktail_ref, vtail_ref, bc_ref, bn_ref,
                 o_ref, os_ref, ko_ref, vo_ref, k16_ref, v16_ref, *, t_len, groups):
    step = pl.program_id(0) * pl.num_programs(1) + pl.program_id(1)
    _sample_attention(step // groups, qs_ref, kn_ref, vn_ref, kc_ref, vc_ref, ktail_ref, vtail_ref,
                      bc_ref, bn_ref, os_ref, ko_ref, vo_ref, t_len=t_len)
    _prompt_attention(q_ref, kt_ref, vt_ref, bias_ref, o_ref, k16_ref, v16_ref)


def _attention(q, k_t, v_t, q_s, k_new, v_new, cache_kt, cache_vt, k_tail, v_tail, *, t_len):
    bsz, s_len, bw = q.shape
    n_seq, groups, fw, wb = cache_kt.shape
    n_pairs = bw // LANES
    assert s_len % ATTN_TILE == 0 and bw % LANES == 0 and bsz * n_pairs == n_seq * groups
    assert n_seq * t_len <= LANES
    r = np.arange(ATTN_TILE)[:, None]
    c = np.arange(s_len)[None, :]
    bias = jnp.asarray(_log_multiplicity(r - (c - (s_len - ATTN_TILE))))
    t = np.repeat(np.arange(t_len), SUBLANES)[:, None]
    bias_c = jnp.asarray(_log_multiplicity(wb + t - np.arange(wb)[None, :]))
    tn = np.arange(SUBLANES)[None, :]
    bias_n = jnp.asarray(np.where(tn < t_len, _log_multiplicity(t - tn), NEG_INF).astype(np.float32))

    tok = pl.BlockSpec((1, s_len, LANES), lambda b, h: (b, 0, h))
    feat = pl.BlockSpec((1, LANES, s_len), lambda b, h: (b, h, 0))
    piece = lambda b, h: ((b * n_pairs + h) // groups, (b * n_pairs + h) % groups, 0, 0)
    small = pl.BlockSpec((1, 1, SUBLANES, fw), piece)
    big = pl.BlockSpec((1, 1, fw, wb), piece)
    tail = pl.BlockSpec((1, fw, LANES), lambda b, h: ((b * n_pairs + h) % groups, 0, 0))
    return pl.pallas_call(
        functools.partial(_attn_kernel, t_len=t_len, groups=groups),
        grid=(bsz, n_pairs),
        in_specs=[tok, feat, feat, _const_spec(bias.shape),
                  small, small, small, big, big, tail, tail,
                  _const_spec(bias_c.shape), _const_spec(bias_n.shape)],
        out_specs=[tok, small, big, big],
        out_shape=[jax.ShapeDtypeStruct((bsz, s_len, bw), BF16),
                   jax.ShapeDtypeStruct(q_s.shape, F32),
                   jax.ShapeDtypeStruct(cache_kt.shape, cache_kt.dtype),
                   jax.ShapeDtypeStruct(cache_vt.shape, cache_vt.dtype)],
        scratch_shapes=[pltpu.VMEM((LANES, s_len), BF16), pltpu.VMEM((2, LANES, s_len), BF16)],
        compiler_params=_params(("parallel", "parallel")),
        name="attn",
    )(q, k_t, v_t, bias, q_s, k_new, v_new, cache_kt, cache_vt, k_tail, v_tail, bias_c, bias_n)


N_POST_WEIGHTS = 11


def _interleave(main, side, ratio):
    for _ in main:
        for _ in range(ratio):
            next(side, None)
    for _ in side:
        pass


def _post_kernel(*refs, stride, ff_piece, down_chunk, gla_chunk):
    x_ref, oa_ref, ob_ref, p_ref, cin_ref = refs[:5]
    (wo_ref, gf_ref, wg_ref, wu_ref, cw_ref, cb_ref, wd_ref, gp_ref, wpg_ref, wpp_ref,
     gl_ref) = refs[5:5 + N_POST_WEIGHTS]
    rest = refs[5 + N_POST_WEIGHTS:]
    if gla_chunk:
        (qa_ref, fa_ref, va_ref, ga_ref, s0_ref, on_ref, y_ref, cout_ref, sa_ref,
         prev_ref, act_ref, oa_scr, b_scr) = rest
    else:
        y_ref, cout_ref, prev_ref, act_ref = rest
    ts = x_ref.shape[1]
    d_ff = wg_ref.shape[1]
    aw = oa_ref.shape[2]
    n_prev = prev_ref.shape[0]
    b_idx, j = pl.program_id(0), pl.program_id(1)

    @pl.when(j == 0)
    def _():
        prev_ref[...] = cin_ref[0]

    if gla_chunk:
        n_heads = oa_scr.shape[1]

        @pl.when((b_idx == 0) & (j == 0))
        def _():
            oa_scr[0] = jnp.zeros(oa_scr.shape[1:], oa_scr.dtype)

        tile_rows = pl.ds(pl.multiple_of(j * ts, ts), ts)
        oa_here = jnp.concatenate([oa_scr[b_idx % 2, h, tile_rows, :] for h in range(n_heads)], axis=1)
        oa = jnp.where(b_idx == 0, oa_ref[0], oa_here)
    else:
        oa = oa_ref[0]

    def post_steps():
        mix = jnp.dot(oa, wo_ref[0:aw, :], preferred_element_type=F32)
        yield
        mix = mix + jnp.dot(ob_ref[0], wo_ref[aw:, :], preferred_element_type=F32)
        yield
        h = x_ref[0] + mix
        hn = _rmsnorm(h, gf_ref[...]).astype(BF16)

        sub = lax.broadcasted_iota(jnp.int32, (SUBLANES, 1), 0)
        for c0 in range(0, d_ff, ff_piece):
            cs = slice(c0, c0 + ff_piece)
            u = jnp.dot(hn, wg_ref[:, cs], preferred_element_type=F32)
            yield
            up = jnp.dot(hn, wu_ref[:, cs], preferred_element_type=F32)
            yield
            prev = prev_ref[:, cs]
            if stride % SUBLANES == 0:
                u1 = jnp.concatenate([prev[stride:], u[:ts - stride]], axis=0)
                u2 = jnp.concatenate([prev, u[:ts - 2 * stride]], axis=0)
            else:
                assert stride == 1 and n_prev == SUBLANES
                r1, r2 = pltpu.roll(u, 1, 0), pltpu.roll(u, 2, 0)
                head1 = jnp.where(sub < 1, pltpu.roll(prev, 1, 0), r1[:SUBLANES])
                head2 = jnp.where(sub < 2, pltpu.roll(prev, 2, 0), r2[:SUBLANES])
                u1 = jnp.concatenate([head1, r1[SUBLANES:]], axis=0)
                u2 = jnp.concatenate([head2, r2[SUBLANES:]], axis=0)
            prev_ref[:, cs] = u[ts - n_prev:]
            conv = cb_ref[:, cs] + cw_ref[0:1, cs] * u2 + cw_ref[1:2, cs] * u1 + cw_ref[2:3, cs] * u
            act_ref[:, cs] = (_silu(conv) * up).astype(BF16)
        cout_ref[0] = prev_ref[...]

        for k0 in range(0, d_ff, down_chunk):
            ks = slice(k0, min(k0 + down_chunk, d_ff))
            down = jnp.dot(act_ref[:, ks], wd_ref[ks, :], preferred_element_type=F32)
            if k0 == 0:
                y_ref[0] = down
            else:
                y_ref[0] += down
            yield

        h = h + y_ref[0]
        gate = jnp.dot(_rmsnorm(h, gp_ref[...]).astype(BF16), wpg_ref[...], preferred_element_type=F32)
        yield
        emb = jnp.dot(p_ref[0].astype(BF16), wpp_ref[...], preferred_element_type=F32)
        yield
        h = h + _sigmoid(gate) * emb
        y_ref[0] = _rmsnorm(h, gl_ref[...])

    def gla_steps():
        halves, level = _gla_levels(gla_chunk)
        onorm_g = on_ref[...]
        st = s0_ref[0, 0].T
        n_chunks = qa_ref.shape[1] // gla_chunk
        rows = lambda c: slice(c * gla_chunk, (c + 1) * gla_chunk)

        def scores_of(c, out):
            return _gla_scores_steps(qa_ref[0, rows(c), :], fa_ref[0, rows(c), :], b_scr.at[c], halves, level, out)

        scores = []
        yield from scores_of(0, scores)
        for c in range(n_chunks):
            ahead = []
            if c + 1 < n_chunks:
                yield from scores_of(c + 1, ahead)
            out = []
            yield from _gla_output_steps(scores, va_ref[0, rows(c), :], ga_ref[0, rows(c), :], st,
                                         b_scr.at[c], onorm_g, out)
            o, st = out
            oa_scr[(b_idx + 1) % 2, j, rows(c), :] = o.astype(oa_scr.dtype)
            scores = ahead
        sa_ref[0, 0] = st.T

    if gla_chunk:
        n_main = 4 + 2 * (d_ff // ff_piece) + -(-d_ff // down_chunk)
        n_side = (qa_ref.shape[1] // gla_chunk) * (gla_chunk.bit_length() + 3)
        _interleave(post_steps(), gla_steps(), -(-n_side // n_main))
    else:
        for _ in post_steps():
            pass


def _post(x, oa, ob, p, conv_in, w_o, g_ffn, w_gate, w_up, conv_w, conv_b, w_down, g_ple, w_pg, w_pp,
          g_fin, *, ts, stride, gla=None):
    bsz, t_len, d = x.shape
    d_ff = w_gate.shape[1]
    n_prev = conv_in.shape[1]
    nt = t_len // ts
    ff_piece, down_chunk = MXU_WIDTH, 3 * MXU_WIDTH
    assert d_ff % ff_piece == 0 and t_len % ts == 0 and ts >= 2 * stride
    tok = lambda b, j: (b, j, 0)
    seq = lambda b, j: (b, 0, 0)
    row = lambda a: a.reshape(1, -1)
    weights = [w_o, row(g_ffn), w_gate, w_up, conv_w, row(conv_b), w_down, row(g_ple), w_pg, w_pp, row(g_fin)]
    assert len(weights) == N_POST_WEIGHTS
    in_specs = [pl.BlockSpec((1, ts, d), tok),
                pl.BlockSpec((1, ts, oa.shape[2]), tok),
                pl.BlockSpec((1, ts, ob.shape[2]), tok),
                pl.BlockSpec((1, ts, p.shape[2]), tok),
                pl.BlockSpec((1, n_prev, d_ff), seq)] + [_const_spec(w.shape) for w in weights]
    out_specs = [pl.BlockSpec((1, ts, d), tok), pl.BlockSpec((1, n_prev, d_ff), seq)]
    out_shape = [jax.ShapeDtypeStruct((bsz, t_len, d), F32), jax.ShapeDtypeStruct((bsz, n_prev, d_ff), F32)]
    scratch = [pltpu.VMEM((n_prev, d_ff), F32), pltpu.VMEM((ts, d_ff), BF16)]
    operands = [x, oa, ob, p, conv_in, *weights]
    gla_chunk = None
    if gla is not None:
        a_part, s0, onorm_g, gla_chunk = gla
        nh, hd = s0.shape[1], HGRN_HEAD_DIM
        assert nt == nh and t_len % gla_chunk == 0 and oa.shape[0] == 1
        nxt = lambda b: jnp.minimum(b + 1, bsz - 1)
        in_specs[1] = pl.BlockSpec((1, ts, oa.shape[2]), lambda b, j: (0, jnp.where(b == 0, j, 0), 0))
        col = lambda part: pl.BlockSpec((1, t_len, hd), lambda b, j, part=part: (nxt(b), 0, part * nh + j))
        state = pl.BlockSpec((1, 1, hd, hd), lambda b, j: (nxt(b), j, 0, 0))
        in_specs += [col(0), col(1), col(2), col(3), state, pl.BlockSpec((1, hd), lambda b, j: (0, j))]
        out_specs.append(pl.BlockSpec((1, 1, hd, hd), lambda b, j: (b, j, 0, 0)))
        out_shape.append(jax.ShapeDtypeStruct(s0.shape, F32))
        scratch += [pltpu.VMEM((2, nh, t_len, hd), BF16), pltpu.VMEM((t_len // gla_chunk, gla_chunk, hd), F32)]
        operands += [a_part, a_part, a_part, a_part, s0, onorm_g.reshape(1, -1)]
    return pl.pallas_call(
        functools.partial(_post_kernel, stride=stride, ff_piece=ff_piece, down_chunk=down_chunk,
                          gla_chunk=gla_chunk),
        grid=(bsz, nt),
        in_specs=in_specs,
        out_specs=out_specs,
        out_shape=out_shape,
        scratch_shapes=scratch,
        compiler_params=_params(("arbitrary", "arbitrary")),
        name="post",
    )(*operands)


PROMPT_TILE = 512
GLA_CHUNK = 128
GLA_GROUP = 8
SAMPLE_SEQS = 8
SAMPLE_HEAD_GROUPS = 2


def _feature_major(cache):
    bsz, wb, nh, hd = cache.shape
    return cache.transpose(0, 2, 3, 1).reshape(bsz, nh * hd, wb)


def _window_major(cache_t, nh):
    bsz, bw, wb = cache_t.shape
    return cache_t.reshape(bsz, nh, bw // nh, wb).transpose(0, 3, 1, 2)


def kernel(x_prompt, x_sample, state_hgrn, cache_win_k, cache_win_v, state_ffn_conv, p_prompt, p_sample,
           norm_attn_g, w_in, hgrn_lb_logits, hgrn_onorm_g, w_o, norm_ffn_g, w_gate, w_up, conv_w, conv_b,
           w_down, norm_ple_g, w_ple_gate, w_ple_proj, norm_final_g):
    depth = w_in.shape[0]
    assert depth == 1, "single-layer step"
    i = 0
    bsz, s_len, d = x_prompt.shape
    dbs, t_dec, _ = x_sample.shape
    nh_a, hd = state_hgrn.shape[2], state_hgrn.shape[3]
    wb, nh_b = cache_win_k.shape[2], cache_win_k.shape[3]
    bw = nh_b * HD_B
    aw = nh_a * hd
    d_ff = w_gate.shape[2]
    assert wb == min(WIN_MAX, PAST_LEN) and s_len <= WIN_MAX and t_dec <= SUBLANES
    n_conv = CONV_W - 1

    w_in16 = w_in[i].astype(BF16)
    w_aq16, w_kv16_t = w_in16[:, :4 * aw + bw], w_in16[:, 4 * aw + bw:].T
    w_gate16, w_up16, w_down16 = w_gate[i].astype(BF16), w_up[i].astype(BF16), w_down[i].astype(BF16)
    post_w = (w_o[i].astype(BF16), norm_ffn_g[i], w_gate16, w_up16, conv_w[i], conv_b[i], w_down16,
              norm_ple_g[i], w_ple_gate[i].astype(BF16), w_ple_proj[i].astype(BF16), norm_final_g)

    a_p, qb_p, kt_p, vt_p = _proj(x_prompt, norm_attn_g[i], w_aq16, w_kv16_t, hgrn_lb_logits,
                                  _rotary_tables(jnp.arange(s_len, dtype=F32)), layer=i, ts=PROMPT_TILE)
    n_tok = dbs * t_dec
    pos_s = jnp.tile(float(PAST_LEN) + jnp.arange(t_dec, dtype=F32), dbs)
    a_s, qb_s, kt_s, vt_s = _proj(x_sample.reshape(1, n_tok, d), norm_attn_g[i], w_aq16, w_kv16_t,
                                  hgrn_lb_logits, _rotary_tables(pos_s), layer=i, ts=n_tok)

    def per_seq(a, fill=0.0):
        a = a.reshape(dbs, t_dec, a.shape[-1]).astype(F32)
        return jnp.pad(a, ((0, 0), (0, SUBLANES - t_dec), (0, 0)), constant_values=fill)

    def tile_rows(a):
        return a[:, :t_dec].transpose(1, 0, 2).reshape(1, n_tok, a.shape[-1])

    s0_p = jnp.zeros((bsz, nh_a, hd, hd), F32)
    gla_chunk = min(GLA_CHUNK, s_len)
    oa_first, s_first = _gla(a_p, s0_p, hgrn_onorm_g[i], chunk=gla_chunk, group=GLA_GROUP, n_seq=1, n_batches=1)
    a_s = a_s[0]
    a_s = jnp.concatenate([per_seq(a_s[:, :aw]), per_seq(a_s[:, aw:2 * aw], 1.0), per_seq(a_s[:, 2 * aw:])],
                          axis=-1)
    oa_s, s_s = _gla(a_s, state_hgrn[i].astype(F32), hgrn_onorm_g[i], chunk=SUBLANES, group=1,
                     n_seq=SAMPLE_SEQS)

    grp = SAMPLE_HEAD_GROUPS
    fw = bw // grp

    def head_groups(a):
        return a.reshape(dbs, SUBLANES, grp, fw).transpose(0, 2, 1, 3)

    def tail(a_t):
        return jnp.pad(a_t.reshape(grp, fw, n_tok), ((0, 0), (0, 0), (0, LANES - n_tok)))

    ob_p, ob_s, kt_new, vt_new = _attention(
        qb_p, kt_p, vt_p,
        head_groups(per_seq(qb_s[0])), head_groups(per_seq(kt_s[0].T)), head_groups(per_seq(vt_s[0].T)),
        _feature_major(cache_win_k[i]).reshape(dbs, grp, fw, wb),
        _feature_major(cache_win_v[i]).reshape(dbs, grp, fw, wb),
        tail(kt_s[0]), tail(vt_s[0]), t_len=t_dec)
    ob_s = ob_s.transpose(0, 2, 1, 3).reshape(dbs, SUBLANES, bw)
    kt_new, vt_new = kt_new.reshape(dbs, bw, wb), vt_new.reshape(dbs, bw, wb)

    y_p, conv_p, s_rest = _post(x_prompt, oa_first, ob_p, p_prompt[i], jnp.zeros((bsz, SUBLANES, d_ff), F32),
                                *post_w, ts=PROMPT_TILE, stride=1,
                                gla=(a_p, s0_p, hgrn_onorm_g[i], gla_chunk))
    s_p = jnp.concatenate([s_first, s_rest[:bsz - 1]], axis=0)
    conv_in_s = state_ffn_conv[i].transpose(1, 0, 2).reshape(1, n_conv * dbs, d_ff)
    p_s = p_sample[i].transpose(1, 0, 2).reshape(1, n_tok, -1)
    x_s = x_sample.transpose(1, 0, 2).reshape(1, n_tok, d)
    y_s, conv_s = _post(x_s, tile_rows(oa_s), tile_rows(ob_s).astype(BF16), p_s, conv_in_s, *post_w,
                        ts=n_tok, stride=dbs)
    keep = min(WIN_MAX, s_len)

    return (y_p,
            y_s.reshape(t_dec, dbs, d).transpose(1, 0, 2),
            s_p[None],
            _window_major(kt_p, nh_b)[None, :, s_len - keep:],
            _window_major(vt_p, nh_b)[None, :, s_len - keep:],
            conv_p[None, :, SUBLANES - n_conv:],
            s_s[None],
            _window_major(kt_new, nh_b)[None],
            _window_major(vt_new, nh_b)[None],
            conv_s.reshape(n_conv, dbs, d_ff).transpose(1, 0, 2)[None])
```

```python
import functools

import numpy as np
import jax
import jax.numpy as jnp
from jax import lax
from jax.experimental import pallas as pl
from jax.experimental.pallas import tpu as pltpu

F32 = jnp.float32
BF16 = jnp.bfloat16

HGRN_HEAD_DIM = 128
HD_B = 64
ROT_DIM = HD_B // 4
ROPE_THETA = 500000.0
DILATIONS = ((128, 1), (512, 4), (2048, 16))
WIN_MAX = 2048
PAST_LEN = 16384
CONV_W = 3
EPS = 1e-6
NEG_INF = -1e30

SUBLANES = 8
LANES = 128
MXU_WIDTH = 256
VMEM_LIMIT_BYTES = 56 * 1024 * 1024

Q_SCALE = (HD_B ** -0.5) * float(np.log2(np.e))

NT_DIMS = (((1,), (1,)), ((), ()))
TN_DIMS = (((0,), (0,)), ((), ()))


def _rmsnorm(x, g):
    return x * lax.rsqrt(jnp.mean(x * x, axis=-1, keepdims=True) + EPS) * g


def _sigmoid(x):
    return 1.0 / (1.0 + jnp.exp(-x))


def _silu(x):
    return x * _sigmoid(x)


def _const_spec(shape):
    nd = len(shape)
    return pl.BlockSpec(shape, lambda *_: (0,) * nd, pipeline_mode=pl.Buffered(1))


def _params(semantics):
    return pltpu.CompilerParams(dimension_semantics=semantics, vmem_limit_bytes=VMEM_LIMIT_BYTES)


def _proj_kernel(x_ref, g_ref, w_ref, wkv_ref, lbl_ref, cos_ref, sin_ref, cost_ref, sint_ref,
                 a_ref, qb_ref, kt_ref, vt_ref, *, layer, a_width, b_width):
    x = x_ref[0]
    a = _rmsnorm(x, g_ref[...]).astype(BF16)

    def pj(j0, width):
        return jnp.dot(a, w_ref[:, j0:j0 + width], preferred_element_type=F32)

    lbl = lbl_ref[...]
    e = jnp.exp(lbl - jnp.max(lbl, axis=0, keepdims=True))
    lb = jnp.sum(e[:layer + 1], axis=0, keepdims=True) / jnp.sum(e, axis=0, keepdims=True)

    aw = a_width
    a_ref[0, :, 0:aw] = _silu(pj(0, aw)) * (HGRN_HEAD_DIM ** -0.5)
    a_ref[0, :, aw:2 * aw] = lb + (1.0 - lb) * _sigmoid(pj(aw, aw))
    a_ref[0, :, 2 * aw:3 * aw] = pj(2 * aw, aw)
    a_ref[0, :, 3 * aw:4 * aw] = _silu(pj(3 * aw, aw))

    half = ROT_DIM // 2
    lane = lax.broadcasted_iota(jnp.int32, (1, LANES), 1) % HD_B
    first_half = lane < half
    cos = cos_ref[...]
    sin = sin_ref[...]
    bw = b_width
    q = pj(4 * aw, bw)
    for c in range(bw // LANES):
        sl = slice(c * LANES, (c + 1) * LANES)
        t = q[:, sl]
        up = pltpu.roll(t, LANES - half, 1)
        dn = pltpu.roll(t, half, 1)
        rot = t * cos + jnp.where(first_half, up, dn) * sin
        qb_ref[0, :, sl] = (rot * Q_SCALE).astype(qb_ref.dtype)

    assert half == SUBLANES
    kt = lax.dot_general(wkv_ref[0:bw, :], a, NT_DIMS, preferred_element_type=F32)
    cos_t = cost_ref[...]
    sin_t = sint_ref[...]
    for h in range(bw // HD_B):
        r = h * HD_B
        x1 = kt[r:r + half]
        x2 = kt[r + half:r + ROT_DIM]
        kt_ref[0, r:r + half, :] = x1 * cos_t - x2 * sin_t
        kt_ref[0, r + half:r + ROT_DIM, :] = x2 * cos_t + x1 * sin_t
        kt_ref[0, r + ROT_DIM:r + HD_B, :] = kt[r + ROT_DIM:r + HD_B]
    vt_ref[0] = lax.dot_general(wkv_ref[bw:2 * bw, :], a, NT_DIMS, preferred_element_type=F32)


def _proj(x, g, w_aq, w_kv_t, lb_logits, tables, *, layer, ts):
    bsz, t_len, d = x.shape
    a_width = lb_logits.shape[1]
    b_width = w_kv_t.shape[0] // 2
    cos_n, sin_n, cos_t, sin_t = tables
    nt = t_len // ts
    tok = lambda j, b: (b, j, 0)
    feat = lambda j, b: (b, 0, j)
    return pl.pallas_call(
        functools.partial(_proj_kernel, layer=layer, a_width=a_width, b_width=b_width),
        grid=(nt, bsz),
        in_specs=[
            pl.BlockSpec((1, ts, d), tok),
            _const_spec((1, d)),
            _const_spec(w_aq.shape),
            _const_spec(w_kv_t.shape),
            _const_spec(lb_logits.shape),
            pl.BlockSpec((ts, LANES), lambda j, b: (j, 0)),
            pl.BlockSpec((ts, LANES), lambda j, b: (j, 0)),
            pl.BlockSpec((ROT_DIM // 2, ts), lambda j, b: (0, j)),
            pl.BlockSpec((ROT_DIM // 2, ts), lambda j, b: (0, j)),
        ],
        out_specs=[
            pl.BlockSpec((1, ts, 4 * a_width), tok),
            pl.BlockSpec((1, ts, b_width), tok),
            pl.BlockSpec((1, b_width, ts), feat),
            pl.BlockSpec((1, b_width, ts), feat),
        ],
        out_shape=[
            jax.ShapeDtypeStruct((bsz, t_len, 4 * a_width), F32),
            jax.ShapeDtypeStruct((bsz, t_len, b_width), BF16),
            jax.ShapeDtypeStruct((bsz, b_width, t_len), F32),
            jax.ShapeDtypeStruct((bsz, b_width, t_len), F32),
        ],
        compiler_params=_params(("parallel", "parallel")),
        name="proj",
    )(x, g.reshape(1, d), w_aq, w_kv_t, lb_logits, cos_n, sin_n, cos_t, sin_t)


def _rotary_tables(pos):
    half = ROT_DIM // 2
    inv_freq = jnp.power(ROPE_THETA, -jnp.arange(half, dtype=F32) * (2.0 / ROT_DIM))
    ang = pos[:, None] * inv_freq[None, :]
    cos, sin = jnp.cos(ang), jnp.sin(ang)
    t_len = pos.shape[0]
    ones = jnp.ones((t_len, HD_B - ROT_DIM), F32)
    cos_h = jnp.concatenate([cos, cos, ones], axis=1)
    sin_h = jnp.concatenate([-sin, sin, 0.0 * ones], axis=1)
    reps = LANES // HD_B
    return jnp.tile(cos_h, (1, reps)), jnp.tile(sin_h, (1, reps)), cos.T, sin.T


def _midpoint_rows(b_ref, base, half):
    if 2 * half >= SUBLANES:
        r = (base // (2 * half)) * (2 * half) + half - 1
        return jnp.broadcast_to(b_ref[r:r + 1, :], (SUBLANES, LANES))
    sub = lax.broadcasted_iota(jnp.int32, (SUBLANES, 1), 0)
    piece = None
    for blk in range(SUBLANES // (2 * half)):
        r = base + blk * 2 * half + half - 1
        row = jnp.broadcast_to(b_ref[r:r + 1, :], (SUBLANES, LANES))
        piece = row if piece is None else jnp.where(sub >= blk * 2 * half, row, piece)
    return piece


def _neg_abs(x):
    bits = lax.bitcast_convert_type(x, jnp.uint32) | jnp.uint32(0x80000000)
    return lax.bitcast_convert_type(bits, F32)


def _gla_levels(chunk):
    row = lax.broadcasted_iota(jnp.int32, (chunk, chunk), 0)
    col = lax.broadcasted_iota(jnp.int32, (chunk, chunk), 1)
    differ = row ^ col
    halves = []
    h = 1
    while h < chunk:
        halves.append(h)
        h *= 2
    level = jnp.where(row < col, -1, 0)
    for n, half in enumerate(halves):
        level = jnp.where((row > col) & (differ >= half), n + 1, level)
    return halves, level


def _gla_scores_steps(q, fg, bs_ref, halves, level, out):
    chunk = q.shape[0]
    g = jnp.log2(fg)
    k = 1.0 - fg

    rows1 = lax.broadcasted_iota(jnp.int32, (chunk, 1), 0)
    b = g
    d = 1
    while d < chunk:
        b = b + jnp.where(rows1 >= d, pltpu.roll(b, d, 0), 0.0)
        d *= 2
    bs_ref[...] = b

    zeros = jnp.zeros((SUBLANES, LANES), F32)
    operands = [(q.astype(BF16), k.astype(BF16))]
    for n, half in enumerate(halves):
        q_rows, k_rows = [], []
        for base in range(0, chunk, SUBLANES):
            rs = slice(base, base + SUBLANES)
            mid = _midpoint_rows(bs_ref, base, half)
            if half < SUBLANES:
                e = jnp.exp2(_neg_abs(b[rs] - mid))
                q_rows.append(q[rs] * e)
                k_rows.append(k[rs] * e)
            elif (base // half) % 2:
                q_rows.append(q[rs] * jnp.exp2(b[rs] - mid))
                k_rows.append(zeros)
            else:
                q_rows.append(zeros)
                k_rows.append(k[rs] * jnp.exp2(mid - b[rs]))
        q_lvl = q_rows[0] if len(q_rows) == 1 else jnp.concatenate(q_rows, axis=0)
        k_lvl = k_rows[0] if len(k_rows) == 1 else jnp.concatenate(k_rows, axis=0)
        operands.append((q_lvl.astype(BF16), k_lvl.astype(BF16)))

    a_mat = None
    for n, (q_lvl, k_lvl) in enumerate(operands):
        lvl = lax.dot_general(q_lvl, k_lvl, NT_DIMS, preferred_element_type=F32)
        a_mat = jnp.where(level == n, lvl, 0.0 if a_mat is None else a_mat)
        yield
    out.extend((q, k, b, a_mat))


def _gla_output_steps(scores, v, gate, st, bs_ref, onorm_g, out):
    q, k, b, a_mat = scores
    chunk = q.shape[0]
    v16 = v.astype(BF16)
    o_inter = lax.dot_general((q * jnp.exp2(b)).astype(BF16), st.astype(BF16), NT_DIMS,
                              preferred_element_type=F32)
    yield
    o = o_inter + jnp.dot(a_mat.astype(BF16), v16, preferred_element_type=F32)
    yield
    b_last = bs_ref[chunk - 1:chunk, :]
    kd = (k * jnp.exp2(b_last - b)).astype(BF16)
    st = st * jnp.exp2(b_last) + lax.dot_general(v16, kd, TN_DIMS, preferred_element_type=F32)
    out.append(_rmsnorm(o, onorm_g) * gate)
    out.append(st)
    yield


def _gla_chunk(q, fg, v, gate, st, bs_ref, halves, level, onorm_g):
    scores, out = [], []
    for _ in _gla_scores_steps(q, fg, bs_ref, halves, level, scores):
        pass
    for _ in _gla_output_steps(scores, v, gate, st, bs_ref, onorm_g, out):
        pass
    return out


def _gla_kernel(q_ref, f_ref, v_ref, gs_ref, s0_ref, on_ref, o_ref, s_ref, st_ref, b_ref, *, chunk, group):
    n_seq, t_len = q_ref.shape[0], q_ref.shape[1]
    for sq in range(n_seq):
        st_ref[sq] = s0_ref[sq, 0].T
    halves, level = _gla_levels(chunk)
    onorm_g = on_ref[...]

    def body(c, carry):
        for sq in range(n_seq):
            st = st_ref[sq]
            for u in range(group):
                rows = pl.ds(pl.multiple_of((c * group + u) * chunk, chunk), chunk)
                o, st = _gla_chunk(q_ref[sq, rows, :], f_ref[sq, rows, :], v_ref[sq, rows, :],
                                   gs_ref[sq, rows, :], st, b_ref.at[sq * group + u], halves, level, onorm_g)
                o_ref[sq, rows, :] = o.astype(o_ref.dtype)
            st_ref[sq] = st
        return carry

    lax.fori_loop(0, t_len // (chunk * group), body, 0)
    for sq in range(n_seq):
        s_ref[sq, 0] = st_ref[sq].T


def _gla(a_part, s0, onorm_g, *, chunk, group, n_seq, n_batches=None):
    _, t_len, a4 = a_part.shape
    bsz = a_part.shape[0] if n_batches is None else n_batches
    nh = s0.shape[1]
    hd = HGRN_HEAD_DIM
    assert bsz % n_seq == 0 and t_len % (chunk * group) == 0
    col = lambda part: pl.BlockSpec((n_seq, t_len, hd), lambda b, h, part=part: (b, 0, part * nh + h))
    return pl.pallas_call(
        functools.partial(_gla_kernel, chunk=chunk, group=group),
        grid=(bsz // n_seq, nh),
        in_specs=[col(0), col(1), col(2), col(3),
                  pl.BlockSpec((n_seq, 1, hd, hd), lambda b, h: (b, h, 0, 0)),
                  pl.BlockSpec((1, hd), lambda b, h: (0, h))],
        out_specs=[pl.BlockSpec((n_seq, t_len, hd), lambda b, h: (b, 0, h)),
                   pl.BlockSpec((n_seq, 1, hd, hd), lambda b, h: (b, h, 0, 0))],
        out_shape=[jax.ShapeDtypeStruct((bsz, t_len, a4 // 4), BF16),
                   jax.ShapeDtypeStruct((bsz,) + s0.shape[1:], F32)],
        scratch_shapes=[pltpu.VMEM((n_seq, hd, hd), F32), pltpu.VMEM((n_seq * group, chunk, hd), F32)],
        compiler_params=_params(("parallel", "parallel")),
        name="gla",
    )(a_part, a_part, a_part, a_part, s0, onorm_g.reshape(1, -1))


def _log_multiplicity(dist):
    count = np.zeros(dist.shape, np.float64)
    for window, dil in DILATIONS:
        count += (dist >= 0) & (dist <= window) & (dist % dil == 0)
    return np.where(count > 0, np.log2(np.maximum(count, 1.0)), NEG_INF).astype(np.float32)


ATTN_TILE = 256
SAMPLE_STAGE_AFTER_BLOCK = (2, 6)
ATTN_LOOKAHEAD = 1


def _prompt_attention(q_ref, kt_ref, vt_ref, bias_ref, o_ref, k16_ref, v16_ref):
    s_len = q_ref.shape[1]
    n_tiles = s_len // ATTN_TILE
    k16_ref[...] = kt_ref[0].astype(BF16)
    feat = lax.broadcasted_iota(jnp.int32, (LANES, 1), 0)
    v = vt_ref[0]
    den_lane = [HD_B, 0]
    v16_ref[0] = jnp.where(feat < HD_B, v, jnp.where(feat == den_lane[0], 1.0, 0.0)).astype(BF16)
    v16_ref[1] = jnp.where(feat >= HD_B, v, jnp.where(feat == den_lane[1], 1.0, 0.0)).astype(BF16)
    lane = lax.broadcasted_iota(jnp.int32, (1, LANES), 1)
    heads = [lane < HD_B, lane >= HD_B]
    def probabilities(i, hd):
        n_keys = (i + 1) * ATTN_TILE
        q = q_ref[0, i * ATTN_TILE:(i + 1) * ATTN_TILE, :]
        s = jnp.dot(jnp.where(heads[hd], q, jnp.zeros_like(q)), k16_ref[:, 0:n_keys],
                    preferred_element_type=F32)
        s = s + bias_ref[:, s_len - n_keys:s_len]
        return jnp.exp2(s - jnp.max(s, axis=-1, keepdims=True)).astype(BF16)

    blocks = [(i, hd) for i in range(n_tiles) for hd in range(len(heads))]
    outs = {}
    ahead = [probabilities(*blk) for blk in blocks[:ATTN_LOOKAHEAD]]
    for n, (i, hd) in enumerate(blocks):
        p = ahead.pop(0)
        if n + ATTN_LOOKAHEAD < len(blocks):
            ahead.append(probabilities(*blocks[n + ATTN_LOOKAHEAD]))
        pv = lax.dot_general(p, v16_ref[hd, :, 0:(i + 1) * ATTN_TILE], NT_DIMS, preferred_element_type=F32)
        outs[hd] = pv / pv[:, den_lane[hd]:den_lane[hd] + 1]
        if hd == len(heads) - 1:
            o_ref[0, i * ATTN_TILE:(i + 1) * ATTN_TILE, :] = jnp.where(heads[0], outs[0], outs[1]).astype(o_ref.dtype)
        yield


def _sample_attention(seq, q_ref, kn_ref, vn_ref, kc_ref, vc_ref, ktail_ref, vtail_ref, bc_ref, bn_ref,
                      o_ref, ko_ref, vo_ref, *, t_len):
    fw, wb = kc_ref.shape[2], kc_ref.shape[3]
    nh = fw // HD_B
    assert nh <= SUBLANES and wb % LANES == 0

    lane = lax.broadcasted_iota(jnp.int32, (1, LANES), 1)
    for src, tail, dst in ((kc_ref, ktail_ref, ko_ref), (vc_ref, vtail_ref, vo_ref)):
        cur = pltpu.roll(src[0, 0, :, 0:LANES], LANES - t_len, 1)
        for c in range(wb // LANES):
            if (c + 1) * LANES < wb:
                nxt = pltpu.roll(src[0, 0, :, (c + 1) * LANES:(c + 2) * LANES], LANES - t_len, 1)
            else:
                nxt = pltpu.roll(tail[0], (LANES - t_len - t_len * seq) % LANES, 1)
            dst[0, 0, :, c * LANES:(c + 1) * LANES] = jnp.where(lane < LANES - t_len, cur, nxt)
            cur = nxt

    sub = lax.broadcasted_iota(jnp.int32, (SUBLANES, fw), 0)
    lane_head = lax.broadcasted_iota(jnp.int32, (SUBLANES, fw), 1) // HD_B
    own = sub == lane_head
    qv = q_ref[0, 0]
    q_rows = jnp.concatenate(
        [jnp.where(own, jnp.broadcast_to(qv[t:t + 1, :], (SUBLANES, fw)), 0.0) for t in range(t_len)],
        axis=0).astype(BF16)

    s_c = jnp.dot(q_rows, kc_ref[0, 0].astype(BF16), preferred_element_type=F32) + bc_ref[...]
    s_n = lax.dot_general(q_rows, kn_ref[0, 0].astype(BF16), NT_DIMS, preferred_element_type=F32) + bn_ref[...]
    yield
    m = jnp.maximum(jnp.max(s_c, axis=-1, keepdims=True), jnp.max(s_n, axis=-1, keepdims=True))
    p_c = jnp.exp2(s_c - m)
    p_n = jnp.exp2(s_n - m)
    den = jnp.sum(p_c, axis=-1, keepdims=True) + jnp.sum(p_n, axis=-1, keepdims=True)
    o = lax.dot_general(p_c.astype(BF16), vc_ref[0, 0].astype(BF16), NT_DIMS, preferred_element_type=F32)
    o = o + jnp.dot(p_n.astype(BF16), vn_ref[0, 0].astype(BF16), preferred_element_type=F32)
    o = o / den
    rows = [jnp.sum(jnp.where(own, o[t * SUBLANES:(t + 1) * SUBLANES, :], 0.0), axis=0, keepdims=True)
            for t in range(t_len)]
    rows.append(jnp.zeros((SUBLANES - t_len, fw), F32))
    o_ref[0, 0] = jnp.concatenate(rows, axis=0)


def _attn_kernel(q_ref, kt_ref, vt_ref, bias_ref,
                 qs_ref, kn_ref, vn_ref, kc_ref, vc_ref, ktail_ref, vtail_ref, bc_ref, bn_ref,
                 o_ref, os_ref, ko_ref, vo_ref, k16_ref, v16_ref, *, t_len, groups):
    step = pl.program_id(0) * pl.num_programs(1) + pl.program_id(1)
    sample = _sample_attention(step // groups, qs_ref, kn_ref, vn_ref, kc_ref, vc_ref, ktail_ref, vtail_ref,
                               bc_ref, bn_ref, os_ref, ko_ref, vo_ref, t_len=t_len)
    for n, _ in enumerate(_prompt_attention(q_ref, kt_ref, vt_ref, bias_ref, o_ref, k16_ref, v16_ref)):
        if n in SAMPLE_STAGE_AFTER_BLOCK:
            next(sample, None)
    for _ in sample:
        pass


def _attention(q, k_t, v_t, q_s, k_new, v_new, cache_kt, cache_vt, k_tail, v_tail, *, t_len):
    bsz, s_len, bw = q.shape
    n_seq, groups, fw, wb = cache_kt.shape
    n_pairs = bw // LANES
    assert s_len % ATTN_TILE == 0 and bw % LANES == 0 and bsz * n_pairs == n_seq * groups
    assert n_seq * t_len <= LANES
    r = np.arange(ATTN_TILE)[:, None]
    c = np.arange(s_len)[None, :]
    bias = jnp.asarray(_log_multiplicity(r - (c - (s_len - ATTN_TILE))))
    t = np.repeat(np.arange(t_len), SUBLANES)[:, None]
    bias_c = jnp.asarray(_log_multiplicity(wb + t - np.arange(wb)[None, :]))
    tn = np.arange(SUBLANES)[None, :]
    bias_n = jnp.asarray(np.where(tn < t_len, _log_multiplicity(t - tn), NEG_INF).astype(np.float32))

    tok = pl.BlockSpec((1, s_len, LANES), lambda b, h: (b, 0, h))
    feat = pl.BlockSpec((1, LANES, s_len), lambda b, h: (b, h, 0))
    piece = lambda b, h: ((b * n_pairs + h) // groups, (b * n_pairs + h) % groups, 0, 0)
    small = pl.BlockSpec((1, 1, SUBLANES, fw), piece)
    big = pl.BlockSpec((1, 1, fw, wb), piece)
    tail = pl.BlockSpec((1, fw, LANES), lambda b, h: ((b * n_pairs + h) % groups, 0, 0))
    return pl.pallas_call(
        functools.partial(_attn_kernel, t_len=t_len, groups=groups),
        grid=(bsz, n_pairs),
        in_specs=[tok, feat, feat, _const_spec(bias.shape),
                  small, small, small, big, big, tail, tail,
                  _const_spec(bias_c.shape), _const_spec(bias_n.shape)],
        out_specs=[tok, small, big, big],
        out_shape=[jax.ShapeDtypeStruct((bsz, s_len, bw), BF16),
                   jax.ShapeDtypeStruct(q_s.shape, F32),
                   jax.ShapeDtypeStruct(cache_kt.shape, cache_kt.dtype),
                   jax.ShapeDtypeStruct(cache_vt.shape, cache_vt.dtype)],
        scratch_shapes=[pltpu.VMEM((LANES, s_len), BF16), pltpu.VMEM((2, LANES, s_len), BF16)],
        compiler_params=_params(("parallel", "parallel")),
        name="attn",
    )(q, k_t, v_t, bias, q_s, k_new, v_new, cache_kt, cache_vt, k_tail, v_tail, bias_c, bias_n)


N_POST_WEIGHTS = 11


def _interleave(main, side, ratio):
    for _ in main:
        for _ in range(ratio):
            next(side, None)
    for _ in side:
        pass


def _post_kernel(*refs, stride, ff_piece, gla_chunk):
    x_ref, oa_ref, ob_ref, p_ref, cin_ref = refs[:5]
    (wo_ref, gf_ref, wg_ref, wu_ref, cw_ref, cb_ref, wd_ref, gp_ref, wpg_ref, wpp_ref,
     gl_ref) = refs[5:5 + N_POST_WEIGHTS]
    rest = refs[5 + N_POST_WEIGHTS:]
    if gla_chunk:
        (qa_ref, fa_ref, va_ref, ga_ref, s0_ref, on_ref, y_ref, cout_ref, sa_ref,
         prev_ref, act_ref, oa_scr, b_scr) = rest
    else:
        y_ref, cout_ref, prev_ref, act_ref = rest
    ts = x_ref.shape[1]
    d_ff = wg_ref.shape[1]
    aw = oa_ref.shape[2]
    n_prev = prev_ref.shape[0]
    b_idx, j = pl.program_id(0), pl.program_id(1)

    @pl.when(j == 0)
    def _():
        prev_ref[...] = cin_ref[0]

    if gla_chunk:
        n_heads = oa_scr.shape[1]

        @pl.when((b_idx == 0) & (j == 0))
        def _():
            oa_scr[0] = jnp.zeros(oa_scr.shape[1:], oa_scr.dtype)

        tile_rows = pl.ds(pl.multiple_of(j * ts, ts), ts)
        oa_here = jnp.concatenate([oa_scr[b_idx % 2, h, tile_rows, :] for h in range(n_heads)], axis=1)
        oa = jnp.where(b_idx == 0, oa_ref[0], oa_here)
    else:
        oa = oa_ref[0]

    def post_steps():
        mix = jnp.dot(oa, wo_ref[0:aw, :], preferred_element_type=F32)
        yield
        mix = mix + jnp.dot(ob_ref[0], wo_ref[aw:, :], preferred_element_type=F32)
        yield
        h = x_ref[0] + mix
        hn = _rmsnorm(h, gf_ref[...]).astype(BF16)

        sub = lax.broadcasted_iota(jnp.int32, (SUBLANES, 1), 0)
        for c0 in range(0, d_ff, ff_piece):
            cs = slice(c0, c0 + ff_piece)
            u = jnp.dot(hn, wg_ref[:, cs], preferred_element_type=F32)
            yield
            up = jnp.dot(hn, wu_ref[:, cs], preferred_element_type=F32)
            yield
            prev = prev_ref[:, cs]
            if stride % SUBLANES == 0:
                u1 = jnp.concatenate([prev[stride:], u[:ts - stride]], axis=0)
                u2 = jnp.concatenate([prev, u[:ts - 2 * stride]], axis=0)
            else:
                assert stride == 1 and n_prev == SUBLANES
                r1, r2 = pltpu.roll(u, 1, 0), pltpu.roll(u, 2, 0)
                head1 = jnp.where(sub < 1, pltpu.roll(prev, 1, 0), r1[:SUBLANES])
                head2 = jnp.where(sub < 2, pltpu.roll(prev, 2, 0), r2[:SUBLANES])
                u1 = jnp.concatenate([head1, r1[SUBLANES:]], axis=0)
                u2 = jnp.concatenate([head2, r2[SUBLANES:]], axis=0)
            prev_ref[:, cs] = u[ts - n_prev:]
            conv = cb_ref[:, cs] + cw_ref[0:1, cs] * u2 + cw_ref[1:2, cs] * u1 + cw_ref[2:3, cs] * u
            act_ref[:, cs] = (_silu(conv) * up).astype(BF16)
        cout_ref[0] = prev_ref[...]

        h = h + jnp.dot(act_ref[...], wd_ref[...], preferred_element_type=F32)
        yield
        gate = jnp.dot(_rmsnorm(h, gp_ref[...]).astype(BF16), wpg_ref[...], preferred_element_type=F32)
        yield
        emb = jnp.dot(p_ref[0].astype(BF16), wpp_ref[...], preferred_element_type=F32)
        yield
        h = h + _sigmoid(gate) * emb
        y_ref[0] = _rmsnorm(h, gl_ref[...])

    def gla_steps():
        halves, level = _gla_levels(gla_chunk)
        onorm_g = on_ref[...]
        st = s0_ref[0, 0].T
        n_chunks = qa_ref.shape[1] // gla_chunk
        rows = lambda c: slice(c * gla_chunk, (c + 1) * gla_chunk)

        def scores_of(c, out):
            return _gla_scores_steps(qa_ref[0, rows(c), :], fa_ref[0, rows(c), :], b_scr.at[c], halves, level, out)

        scores = []
        yield from scores_of(0, scores)
        for c in range(n_chunks):
            ahead = []
            if c + 1 < n_chunks:
                yield from scores_of(c + 1, ahead)
            out = []
            yield from _gla_output_steps(scores, va_ref[0, rows(c), :], ga_ref[0, rows(c), :], st,
                                         b_scr.at[c], onorm_g, out)
            o, st = out
            oa_scr[(b_idx + 1) % 2, j, rows(c), :] = o.astype(oa_scr.dtype)
            scores = ahead
        sa_ref[0, 0] = st.T

    if gla_chunk:
        n_main = 5 + 2 * (d_ff // ff_piece)
        n_side = (qa_ref.shape[1] // gla_chunk) * (gla_chunk.bit_length() + 3)
        _interleave(post_steps(), gla_steps(), -(-n_side // n_main))
    else:
        for _ in post_steps():
            pass


def _post(x, oa, ob, p, conv_in, w_o, g_ffn, w_gate, w_up, conv_w, conv_b, w_down, g_ple, w_pg, w_pp,
          g_fin, *, ts, stride, gla=None):
    bsz, t_len, d = x.shape
    d_ff = w_gate.shape[1]
    n_prev = conv_in.shape[1]
    nt = t_len // ts
    ff_piece = MXU_WIDTH
    assert d_ff % ff_piece == 0 and t_len % ts == 0 and ts >= 2 * stride
    tok = lambda b, j: (b, j, 0)
    seq = lambda b, j: (b, 0, 0)
    row = lambda a: a.reshape(1, -1)
    weights = [w_o, row(g_ffn), w_gate, w_up, conv_w, row(conv_b), w_down, row(g_ple), w_pg, w_pp, row(g_fin)]
    assert len(weights) == N_POST_WEIGHTS
    in_specs = [pl.BlockSpec((1, ts, d), tok),
                pl.BlockSpec((1, ts, oa.shape[2]), tok),
                pl.BlockSpec((1, ts, ob.shape[2]), tok),
                pl.BlockSpec((1, ts, p.shape[2]), tok),
                pl.BlockSpec((1, n_prev, d_ff), seq)] + [_const_spec(w.shape) for w in weights]
    out_specs = [pl.BlockSpec((1, ts, d), tok), pl.BlockSpec((1, n_prev, d_ff), seq)]
    out_shape = [jax.ShapeDtypeStruct((bsz, t_len, d), F32), jax.ShapeDtypeStruct((bsz, n_prev, d_ff), F32)]
    scratch = [pltpu.VMEM((n_prev, d_ff), F32), pltpu.VMEM((ts, d_ff), BF16)]
    operands = [x, oa, ob, p, conv_in, *weights]
    gla_chunk = None
    if gla is not None:
        a_part, s0, onorm_g, gla_chunk = gla
        nh, hd = s0.shape[1], HGRN_HEAD_DIM
        assert nt == nh and t_len % gla_chunk == 0 and oa.shape[0] == 1
        nxt = lambda b: jnp.minimum(b + 1, bsz - 1)
        in_specs[1] = pl.BlockSpec((1, ts, oa.shape[2]), lambda b, j: (0, jnp.where(b == 0, j, 0), 0))
        col = lambda part: pl.BlockSpec((1, t_len, hd), lambda b, j, part=part: (nxt(b), 0, part * nh + j))
        state = pl.BlockSpec((1, 1, hd, hd), lambda b, j: (nxt(b), j, 0, 0))
        in_specs += [col(0), col(1), col(2), col(3), state, pl.BlockSpec((1, hd), lambda b, j: (0, j))]
        out_specs.append(pl.BlockSpec((1, 1, hd, hd), lambda b, j: (b, j, 0, 0)))
        out_shape.append(jax.ShapeDtypeStruct(s0.shape, F32))
        scratch += [pltpu.VMEM((2, nh, t_len, hd), BF16), pltpu.VMEM((t_len // gla_chunk, gla_chunk, hd), F32)]
        operands += [a_part, a_part, a_part, a_part, s0, onorm_g.reshape(1, -1)]
    return pl.pallas_call(
        functools.partial(_post_kernel, stride=stride, ff_piece=ff_piece, gla_chunk=gla_chunk),
        grid=(bsz, nt),
        in_specs=in_specs,
        out_specs=out_specs,
        out_shape=out_shape,
        scratch_shapes=scratch,
        compiler_params=_params(("arbitrary", "arbitrary")),
        name="post",
    )(*operands)


PROMPT_TILE = 512
PROJ_TILE = 1024
GLA_CHUNK = 128
GLA_GROUP = 8
SAMPLE_SEQS = 16
SAMPLE_HEAD_GROUPS = 2


def _feature_major(cache):
    bsz, wb, nh, hd = cache.shape
    return cache.transpose(0, 2, 3, 1).reshape(bsz, nh * hd, wb)


def _window_major(cache_t, nh):
    bsz, bw, wb = cache_t.shape
    return cache_t.reshape(bsz, nh, bw // nh, wb).transpose(0, 3, 1, 2)


def kernel(x_prompt, x_sample, state_hgrn, cache_win_k, cache_win_v, state_ffn_conv, p_prompt, p_sample,
           norm_attn_g, w_in, hgrn_lb_logits, hgrn_onorm_g, w_o, norm_ffn_g, w_gate, w_up, conv_w, conv_b,
           w_down, norm_ple_g, w_ple_gate, w_ple_proj, norm_final_g):
    depth = w_in.shape[0]
    assert depth == 1, "single-layer step"
    i = 0
    bsz, s_len, d = x_prompt.shape
    dbs, t_dec, _ = x_sample.shape
    nh_a, hd = state_hgrn.shape[2], state_hgrn.shape[3]
    wb, nh_b = cache_win_k.shape[2], cache_win_k.shape[3]
    bw = nh_b * HD_B
    aw = nh_a * hd
    d_ff = w_gate.shape[2]
    assert wb == min(WIN_MAX, PAST_LEN) and s_len <= WIN_MAX and t_dec <= SUBLANES
    n_conv = CONV_W - 1

    w_aq16 = w_in[i][:, :4 * aw + bw].astype(BF16)
    w_kv16_t = w_in[i][:, 4 * aw + bw:].T.astype(BF16)
    w_gate16, w_up16, w_down16 = w_gate[i].astype(BF16), w_up[i].astype(BF16), w_down[i].astype(BF16)
    post_w = (w_o[i].astype(BF16), norm_ffn_g[i], w_gate16, w_up16, conv_w[i], conv_b[i], w_down16,
              norm_ple_g[i], w_ple_gate[i].astype(BF16), w_ple_proj[i].astype(BF16), norm_final_g)

    a_p, qb_p, kt_p, vt_p = _proj(x_prompt, norm_attn_g[i], w_aq16, w_kv16_t, hgrn_lb_logits,
                                  _rotary_tables(jnp.arange(s_len, dtype=F32)), layer=i, ts=min(PROJ_TILE, s_len))
    n_tok = dbs * t_dec
    pos_s = jnp.tile(float(PAST_LEN) + jnp.arange(t_dec, dtype=F32), dbs)
    a_s, qb_s, kt_s, vt_s = _proj(x_sample.reshape(1, n_tok, d), norm_attn_g[i], w_aq16, w_kv16_t,
                                  hgrn_lb_logits, _rotary_tables(pos_s), layer=i, ts=n_tok)

    def per_seq(a, fill=0.0):
        a = a.reshape(dbs, t_dec, a.shape[-1]).astype(F32)
        return jnp.pad(a, ((0, 0), (0, SUBLANES - t_dec), (0, 0)), constant_values=fill)

    def tile_rows(a):
        return a[:, :t_dec].transpose(1, 0, 2).reshape(1, n_tok, a.shape[-1])

    s0_p = jnp.zeros((bsz, nh_a, hd, hd), F32)
    gla_chunk = min(GLA_CHUNK, s_len)
    oa_first, s_first = _gla(a_p, s0_p, hgrn_onorm_g[i], chunk=gla_chunk, group=GLA_GROUP, n_seq=1, n_batches=1)
    a_s = a_s[0]
    a_s = jnp.concatenate([per_seq(a_s[:, :aw]), per_seq(a_s[:, aw:2 * aw], 1.0), per_seq(a_s[:, 2 * aw:])],
                          axis=-1)
    oa_s, s_s = _gla(a_s, state_hgrn[i].astype(F32), hgrn_onorm_g[i], chunk=SUBLANES, group=1,
                     n_seq=SAMPLE_SEQS)

    grp = SAMPLE_HEAD_GROUPS
    fw = bw // grp

    def head_groups(a):
        return a.reshape(dbs, SUBLANES, grp, fw).transpose(0, 2, 1, 3)

    def tail(a_t):
        return jnp.pad(a_t.reshape(grp, fw, n_tok), ((0, 0), (0, 0), (0, LANES - n_tok)))

    ob_p, ob_s, kt_new, vt_new = _attention(
        qb_p, kt_p, vt_p,
        head_groups(per_seq(qb_s[0])), head_groups(per_seq(kt_s[0].T)), head_groups(per_seq(vt_s[0].T)),
        _feature_major(cache_win_k[i]).reshape(dbs, grp, fw, wb),
        _feature_major(cache_win_v[i]).reshape(dbs, grp, fw, wb),
        tail(kt_s[0]), tail(vt_s[0]), t_len=t_dec)
    ob_s = ob_s.transpose(0, 2, 1, 3).reshape(dbs, SUBLANES, bw)
    kt_new, vt_new = kt_new.reshape(dbs, bw, wb), vt_new.reshape(dbs, bw, wb)

    y_p, conv_p, s_rest = _post(x_prompt, oa_first, ob_p, p_prompt[i], jnp.zeros((bsz, SUBLANES, d_ff), F32),
                                *post_w, ts=PROMPT_TILE, stride=1,
                                gla=(a_p, s0_p, hgrn_onorm_g[i], gla_chunk))
    s_p = jnp.concatenate([s_first, s_rest[:bsz - 1]], axis=0)
    conv_in_s = state_ffn_conv[i].transpose(1, 0, 2).reshape(1, n_conv * dbs, d_ff)
    p_s = p_sample[i].transpose(1, 0, 2).reshape(1, n_tok, -1)
    x_s = x_sample.transpose(1, 0, 2).reshape(1, n_tok, d)
    y_s, conv_s = _post(x_s, tile_rows(oa_s), tile_rows(ob_s).astype(BF16), p_s, conv_in_s, *post_w,
                        ts=n_tok, stride=dbs)
    keep = min(WIN_MAX, s_len)

    return (y_p,
            y_s.reshape(t_dec, dbs, d).transpose(1, 0, 2),
            s_p[None],
            _window_major(kt_p, nh_b)[None, :, s_len - keep:],
            _window_major(vt_p, nh_b)[None, :, s_len - keep:],
            conv_p[None, :, SUBLANES - n_conv:],
            s_s[None],
            _window_major(kt_new, nh_b)[None],
            _window_major(vt_new, nh_b)[None],
            conv_s.reshape(n_conv, dbs, d_ff).transpose(1, 0, 2)[None])
```

```python
import functools

import numpy as np
import jax
import jax.numpy as jnp
from jax import lax
from jax.experimental import pallas as pl
from jax.experimental.pallas import tpu as pltpu

F32 = jnp.float32
BF16 = jnp.bfloat16

HGRN_HEAD_DIM = 128
HD_B = 64
ROT_DIM = HD_B // 4
ROPE_THETA = 500000.0
DILATIONS = ((128, 1), (512, 4), (2048, 16))
WIN_MAX = 2048
PAST_LEN = 16384
CONV_W = 3
EPS = 1e-6
NEG_INF = -1e30

SUBLANES = 8
LANES = 128
MXU_WIDTH = 256
VMEM_LIMIT_BYTES = 56 * 1024 * 1024

Q_SCALE = (HD_B ** -0.5) * float(np.log2(np.e))

NT_DIMS = (((1,), (1,)), ((), ()))
TN_DIMS = (((0,), (0,)), ((), ()))


def _rmsnorm(x, g):
    return x * lax.rsqrt(jnp.mean(x * x, axis=-1, keepdims=True) + EPS) * g


def _sigmoid(x):
    return 1.0 / (1.0 + jnp.exp(-x))


def _silu(x):
    return x * _sigmoid(x)


def _const_spec(shape):
    nd = len(shape)
    return pl.BlockSpec(shape, lambda *_: (0,) * nd, pipeline_mode=pl.Buffered(1))


def _params(semantics):
    return pltpu.CompilerParams(dimension_semantics=semantics, vmem_limit_bytes=VMEM_LIMIT_BYTES)


def _proj_kernel(x_ref, g_ref, w_ref, wkv_ref, lbl_ref, cos_ref, sin_ref, cost_ref, sint_ref,
                 a_ref, qb_ref, kt_ref, vt_ref, *, layer, a_width, b_width):
    x = x_ref[0]
    a = _rmsnorm(x, g_ref[...]).astype(BF16)

    def pj(j0, width):
        return jnp.dot(a, w_ref[:, j0:j0 + width], preferred_element_type=F32)

    lbl = lbl_ref[...]
    e = jnp.exp(lbl - jnp.max(lbl, axis=0, keepdims=True))
    lb = jnp.sum(e[:layer + 1], axis=0, keepdims=True) / jnp.sum(e, axis=0, keepdims=True)

    aw = a_width
    hd = HGRN_HEAD_DIM
    nh = aw // hd
    parts = (lambda: _silu(pj(0, aw)) * (hd ** -0.5),
             lambda: lb + (1.0 - lb) * _sigmoid(pj(aw, aw)),
             lambda: pj(2 * aw, aw),
             lambda: _silu(pj(3 * aw, aw)))
    for part, value in enumerate(parts):
        val = value()
        for h in range(nh):
            a_ref[0, part * nh + h] = val[:, h * hd:(h + 1) * hd]

    half = ROT_DIM // 2
    lane = lax.broadcasted_iota(jnp.int32, (1, LANES), 1) % HD_B
    first_half = lane < half
    cos = cos_ref[...]
    sin = sin_ref[...]
    bw = b_width
    q = pj(4 * aw, bw)
    for c in range(bw // LANES):
        sl = slice(c * LANES, (c + 1) * LANES)
        t = q[:, sl]
        up = pltpu.roll(t, LANES - half, 1)
        dn = pltpu.roll(t, half, 1)
        rot = t * cos + jnp.where(first_half, up, dn) * sin
        qb_ref[0, :, sl] = (rot * Q_SCALE).astype(qb_ref.dtype)

    assert half == SUBLANES
    kt = lax.dot_general(wkv_ref[0:bw, :], a, NT_DIMS, preferred_element_type=F32)
    cos_t = cost_ref[...]
    sin_t = sint_ref[...]
    for h in range(bw // HD_B):
        r = h * HD_B
        x1 = kt[r:r + half]
        x2 = kt[r + half:r + ROT_DIM]
        kt_ref[0, r:r + half, :] = x1 * cos_t - x2 * sin_t
        kt_ref[0, r + half:r + ROT_DIM, :] = x2 * cos_t + x1 * sin_t
        kt_ref[0, r + ROT_DIM:r + HD_B, :] = kt[r + ROT_DIM:r + HD_B]
    vt_ref[0] = lax.dot_general(wkv_ref[bw:2 * bw, :], a, NT_DIMS, preferred_element_type=F32)


def _proj(x, g, w_aq, w_kv_t, lb_logits, tables, *, layer, ts):
    bsz, t_len, d = x.shape
    a_width = lb_logits.shape[1]
    b_width = w_kv_t.shape[0] // 2
    cos_n, sin_n, cos_t, sin_t = tables
    nt = t_len // ts
    tok = lambda j, b: (b, j, 0)
    feat = lambda j, b: (b, 0, j)
    return pl.pallas_call(
        functools.partial(_proj_kernel, layer=layer, a_width=a_width, b_width=b_width),
        grid=(nt, bsz),
        in_specs=[
            pl.BlockSpec((1, ts, d), tok),
            _const_spec((1, d)),
            _const_spec(w_aq.shape),
            _const_spec(w_kv_t.shape),
            _const_spec(lb_logits.shape),
            pl.BlockSpec((ts, LANES), lambda j, b: (j, 0)),
            pl.BlockSpec((ts, LANES), lambda j, b: (j, 0)),
            pl.BlockSpec((ROT_DIM // 2, ts), lambda j, b: (0, j)),
            pl.BlockSpec((ROT_DIM // 2, ts), lambda j, b: (0, j)),
        ],
        out_specs=[
            pl.BlockSpec((1, 4 * a_width // HGRN_HEAD_DIM, ts, HGRN_HEAD_DIM), lambda j, b: (b, 0, j, 0)),
            pl.BlockSpec((1, ts, b_width), tok),
            pl.BlockSpec((1, b_width, ts), feat),
            pl.BlockSpec((1, b_width, ts), feat),
        ],
        out_shape=[
            jax.ShapeDtypeStruct((bsz, 4 * a_width // HGRN_HEAD_DIM, t_len, HGRN_HEAD_DIM), F32),
            jax.ShapeDtypeStruct((bsz, t_len, b_width), BF16),
            jax.ShapeDtypeStruct((bsz, b_width, t_len), F32),
            jax.ShapeDtypeStruct((bsz, b_width, t_len), F32),
        ],
        compiler_params=_params(("parallel", "parallel")),
        name="proj",
    )(x, g.reshape(1, d), w_aq, w_kv_t, lb_logits, cos_n, sin_n, cos_t, sin_t)


def _rotary_tables(pos):
    half = ROT_DIM // 2
    inv_freq = jnp.power(ROPE_THETA, -jnp.arange(half, dtype=F32) * (2.0 / ROT_DIM))
    ang = pos[:, None] * inv_freq[None, :]
    cos, sin = jnp.cos(ang), jnp.sin(ang)
    t_len = pos.shape[0]
    ones = jnp.ones((t_len, HD_B - ROT_DIM), F32)
    cos_h = jnp.concatenate([cos, cos, ones], axis=1)
    sin_h = jnp.concatenate([-sin, sin, 0.0 * ones], axis=1)
    reps = LANES // HD_B
    return jnp.tile(cos_h, (1, reps)), jnp.tile(sin_h, (1, reps)), cos.T, sin.T


def _midpoint_rows(b_ref, base, half):
    if 2 * half >= SUBLANES:
        r = (base // (2 * half)) * (2 * half) + half - 1
        return jnp.broadcast_to(b_ref[r:r + 1, :], (SUBLANES, LANES))
    sub = lax.broadcasted_iota(jnp.int32, (SUBLANES, 1), 0)
    piece = None
    for blk in range(SUBLANES // (2 * half)):
        r = base + blk * 2 * half + half - 1
        row = jnp.broadcast_to(b_ref[r:r + 1, :], (SUBLANES, LANES))
        piece = row if piece is None else jnp.where(sub >= blk * 2 * half, row, piece)
    return piece


def _neg_abs(x):
    bits = lax.bitcast_convert_type(x, jnp.uint32) | jnp.uint32(0x80000000)
    return lax.bitcast_convert_type(bits, F32)


def _gla_levels(chunk):
    row = lax.broadcasted_iota(jnp.int32, (chunk, chunk), 0)
    col = lax.broadcasted_iota(jnp.int32, (chunk, chunk), 1)
    differ = row ^ col
    halves = []
    h = 1
    while h < chunk:
        halves.append(h)
        h *= 2
    level = jnp.where(row < col, -1, 0)
    for n, half in enumerate(halves):
        level = jnp.where((row > col) & (differ >= half), n + 1, level)
    return halves, level


def _gla_scores_steps(q, fg, bs_ref, halves, level, out):
    chunk = q.shape[0]
    g = jnp.log2(fg)
    k = 1.0 - fg

    rows1 = lax.broadcasted_iota(jnp.int32, (chunk, 1), 0)
    b = g
    d = 1
    while d < chunk:
        b = b + jnp.where(rows1 >= d, pltpu.roll(b, d, 0), 0.0)
        d *= 2
    bs_ref[...] = b

    zeros = jnp.zeros((SUBLANES, LANES), F32)
    operands = [(q.astype(BF16), k.astype(BF16))]
    for n, half in enumerate(halves):
        q_rows, k_rows = [], []
        for base in range(0, chunk, SUBLANES):
            rs = slice(base, base + SUBLANES)
            mid = _midpoint_rows(bs_ref, base, half)
            if half < SUBLANES:
                e = jnp.exp2(_neg_abs(b[rs] - mid))
                q_rows.append(q[rs] * e)
                k_rows.append(k[rs] * e)
            elif (base // half) % 2:
                q_rows.append(q[rs] * jnp.exp2(b[rs] - mid))
                k_rows.append(zeros)
            else:
                q_rows.append(zeros)
                k_rows.append(k[rs] * jnp.exp2(mid - b[rs]))
        q_lvl = q_rows[0] if len(q_rows) == 1 else jnp.concatenate(q_rows, axis=0)
        k_lvl = k_rows[0] if len(k_rows) == 1 else jnp.concatenate(k_rows, axis=0)
        operands.append((q_lvl.astype(BF16), k_lvl.astype(BF16)))

    a_mat = None
    for n, (q_lvl, k_lvl) in enumerate(operands):
        lvl = lax.dot_general(q_lvl, k_lvl, NT_DIMS, preferred_element_type=F32)
        a_mat = jnp.where(level == n, lvl, 0.0 if a_mat is None else a_mat)
        yield
    out.extend((q, k, b, a_mat))


def _gla_output_steps(scores, v, gate, st, bs_ref, onorm_g, out):
    q, k, b, a_mat = scores
    chunk = q.shape[0]
    v16 = v.astype(BF16)
    o_inter = lax.dot_general((q * jnp.exp2(b)).astype(BF16), st.astype(BF16), NT_DIMS,
                              preferred_element_type=F32)
    yield
    o = o_inter + jnp.dot(a_mat.astype(BF16), v16, preferred_element_type=F32)
    yield
    b_last = bs_ref[chunk - 1:chunk, :]
    kd = (k * jnp.exp2(b_last - b)).astype(BF16)
    st = st * jnp.exp2(b_last) + lax.dot_general(v16, kd, TN_DIMS, preferred_element_type=F32)
    out.append(_rmsnorm(o, onorm_g) * gate)
    out.append(st)
    yield


def _gla_chunk(q, fg, v, gate, st, bs_ref, halves, level, onorm_g):
    scores, out = [], []
    for _ in _gla_scores_steps(q, fg, bs_ref, halves, level, scores):
        pass
    for _ in _gla_output_steps(scores, v, gate, st, bs_ref, onorm_g, out):
        pass
    return out


def _gla_kernel(q_ref, f_ref, v_ref, gs_ref, s0_ref, on_ref, o_ref, s_ref, st_ref, b_ref, *, chunk, group):
    n_seq, t_len = q_ref.shape[0], q_ref.shape[1]
    for sq in range(n_seq):
        st_ref[sq] = s0_ref[sq, 0].T
    halves, level = _gla_levels(chunk)
    onorm_g = on_ref[...]

    def body(c, carry):
        for sq in range(n_seq):
            st = st_ref[sq]
            for u in range(group):
                rows = pl.ds(pl.multiple_of((c * group + u) * chunk, chunk), chunk)
                o, st = _gla_chunk(q_ref[sq, rows, :], f_ref[sq, rows, :], v_ref[sq, rows, :],
                                   gs_ref[sq, rows, :], st, b_ref.at[sq * group + u], halves, level, onorm_g)
                o_ref[sq, rows, :] = o.astype(o_ref.dtype)
            st_ref[sq] = st
        return carry

    lax.fori_loop(0, t_len // (chunk * group), body, 0)
    for sq in range(n_seq):
        s_ref[sq, 0] = st_ref[sq].T


def _gla(a_part, s0, onorm_g, *, chunk, group, n_seq, n_batches=None):
    t_len = a_part.shape[2]
    bsz = a_part.shape[0] if n_batches is None else n_batches
    nh = s0.shape[1]
    hd = HGRN_HEAD_DIM
    assert bsz % n_seq == 0 and t_len % (chunk * group) == 0
    col = lambda part: pl.BlockSpec((n_seq, None, t_len, hd), lambda b, h, part=part: (b, part * nh + h, 0, 0))
    return pl.pallas_call(
        functools.partial(_gla_kernel, chunk=chunk, group=group),
        grid=(bsz // n_seq, nh),
        in_specs=[col(0), col(1), col(2), col(3),
                  pl.BlockSpec((n_seq, 1, hd, hd), lambda b, h: (b, h, 0, 0)),
                  pl.BlockSpec((1, hd), lambda b, h: (0, h))],
        out_specs=[pl.BlockSpec((n_seq, t_len, hd), lambda b, h: (b, 0, h)),
                   pl.BlockSpec((n_seq, 1, hd, hd), lambda b, h: (b, h, 0, 0))],
        out_shape=[jax.ShapeDtypeStruct((bsz, t_len, nh * hd), BF16),
                   jax.ShapeDtypeStruct((bsz,) + s0.shape[1:], F32)],
        scratch_shapes=[pltpu.VMEM((n_seq, hd, hd), F32), pltpu.VMEM((n_seq * group, chunk, hd), F32)],
        compiler_params=_params(("parallel", "parallel")),
        name="gla",
    )(a_part, a_part, a_part, a_part, s0, onorm_g.reshape(1, -1))


def _log_multiplicity(dist):
    count = np.zeros(dist.shape, np.float64)
    for window, dil in DILATIONS:
        count += (dist >= 0) & (dist <= window) & (dist % dil == 0)
    return np.where(count > 0, np.log2(np.maximum(count, 1.0)), NEG_INF).astype(np.float32)


ATTN_TILE = 256
SAMPLE_STAGE_AFTER_BLOCK = (2, 6)
ATTN_LOOKAHEAD = 1


def _prompt_attention(q_ref, kt_ref, vt_ref, bias_ref, o_ref, k16_ref, v16_ref):
    s_len = q_ref.shape[1]
    n_tiles = s_len // ATTN_TILE
    k16_ref[...] = kt_ref[0].astype(BF16)
    feat = lax.broadcasted_iota(jnp.int32, (LANES, 1), 0)
    v = vt_ref[0]
    den_lane = [HD_B, 0]
    v16_ref[0] = jnp.where(feat < HD_B, v, jnp.where(feat == den_lane[0], 1.0, 0.0)).astype(BF16)
    v16_ref[1] = jnp.where(feat >= HD_B, v, jnp.where(feat == den_lane[1], 1.0, 0.0)).astype(BF16)
    lane = lax.broadcasted_iota(jnp.int32, (1, LANES), 1)
    heads = [lane < HD_B, lane >= HD_B]
    def probabilities(i, hd):
        n_keys = (i + 1) * ATTN_TILE
        q = q_ref[0, i * ATTN_TILE:(i + 1) * ATTN_TILE, :]
        s = jnp.dot(jnp.where(heads[hd], q, jnp.zeros_like(q)), k16_ref[:, 0:n_keys],
                    preferred_element_type=F32)
        s = s + bias_ref[:, s_len - n_keys:s_len]
        return jnp.exp2(s - jnp.max(s, axis=-1, keepdims=True)).astype(BF16)

    blocks = [(i, hd) for i in range(n_tiles) for hd in range(len(heads))]
    outs = {}
    ahead = [probabilities(*blk) for blk in blocks[:ATTN_LOOKAHEAD]]
    for n, (i, hd) in enumerate(blocks):
        p = ahead.pop(0)
        if n + ATTN_LOOKAHEAD < len(blocks):
            ahead.append(probabilities(*blocks[n + ATTN_LOOKAHEAD]))
        pv = lax.dot_general(p, v16_ref[hd, :, 0:(i + 1) * ATTN_TILE], NT_DIMS, preferred_element_type=F32)
        outs[hd] = pv / pv[:, den_lane[hd]:den_lane[hd] + 1]
        if hd == len(heads) - 1:
            o_ref[0, i * ATTN_TILE:(i + 1) * ATTN_TILE, :] = jnp.where(heads[0], outs[0], outs[1]).astype(o_ref.dtype)
        yield


def _sample_attention(seq, q_ref, kn_ref, vn_ref, kc_ref, vc_ref, ktail_ref, vtail_ref, bc_ref, bn_ref,
                      o_ref, ko_ref, vo_ref, *, t_len):
    fw, wb = kc_ref.shape[2], kc_ref.shape[3]
    nh = fw // HD_B
    assert nh <= SUBLANES and wb % LANES == 0

    lane = lax.broadcasted_iota(jnp.int32, (1, LANES), 1)
    for src, tail, dst in ((kc_ref, ktail_ref, ko_ref), (vc_ref, vtail_ref, vo_ref)):
        cur = pltpu.roll(src[0, 0, :, 0:LANES], LANES - t_len, 1)
        for c in range(wb // LANES):
            if (c + 1) * LANES < wb:
                nxt = pltpu.roll(src[0, 0, :, (c + 1) * LANES:(c + 2) * LANES], LANES - t_len, 1)
            else:
                nxt = pltpu.roll(tail[0], (LANES - t_len - t_len * seq) % LANES, 1)
            dst[0, 0, :, c * LANES:(c + 1) * LANES] = jnp.where(lane < LANES - t_len, cur, nxt)
            cur = nxt

    sub = lax.broadcasted_iota(jnp.int32, (SUBLANES, fw), 0)
    lane_head = lax.broadcasted_iota(jnp.int32, (SUBLANES, fw), 1) // HD_B
    own = sub == lane_head
    qv = q_ref[0, 0]
    q_rows = jnp.concatenate(
        [jnp.where(own, jnp.broadcast_to(qv[t:t + 1, :], (SUBLANES, fw)), 0.0) for t in range(t_len)],
        axis=0).astype(BF16)

    s_c = jnp.dot(q_rows, kc_ref[0, 0].astype(BF16), preferred_element_type=F32) + bc_ref[...]
    s_n = lax.dot_general(q_rows, kn_ref[0, 0].astype(BF16), NT_DIMS, preferred_element_type=F32) + bn_ref[...]
    yield
    m = jnp.maximum(jnp.max(s_c, axis=-1, keepdims=True), jnp.max(s_n, axis=-1, keepdims=True))
    p_c = jnp.exp2(s_c - m)
    p_n = jnp.exp2(s_n - m)
    den = jnp.sum(p_c, axis=-1, keepdims=True) + jnp.sum(p_n, axis=-1, keepdims=True)
    o = lax.dot_general(p_c.astype(BF16), vc_ref[0, 0].astype(BF16), NT_DIMS, preferred_element_type=F32)
    o = o + jnp.dot(p_n.astype(BF16), vn_ref[0, 0].astype(BF16), preferred_element_type=F32)
    o = o / den
    rows = [jnp.sum(jnp.where(own, o[t * SUBLANES:(t + 1) * SUBLANES, :], 0.0), axis=0, keepdims=True)
            for t in range(t_len)]
    rows.append(jnp.zeros((SUBLANES - t_len, fw), F32))
    o_ref[0, 0] = jnp.concatenate(rows, axis=0)


def _attn_kernel(q_ref, kt_ref, vt_ref, bias_ref,
                 qs_ref, kn_ref, vn_ref, kc_ref, vc_ref, ktail_ref, vtail_ref, bc_ref, bn_ref,
                 o_ref, os_ref, ko_ref, vo_ref, k16_ref, v16_ref, *, t_len, groups):
    step = pl.program_id(0) * pl.num_programs(1) + pl.program_id(1)
    sample = _sample_attention(step // groups, qs_ref, kn_ref, vn_ref, kc_ref, vc_ref, ktail_ref, vtail_ref,
                               bc_ref, bn_ref, os_ref, ko_ref, vo_ref, t_len=t_len)
    for n, _ in enumerate(_prompt_attention(q_ref, kt_ref, vt_ref, bias_ref, o_ref, k16_ref, v16_ref)):
        if n in SAMPLE_STAGE_AFTER_BLOCK:
            next(sample, None)
    for _ in sample:
        pass


def _attention(q, k_t, v_t, q_s, k_new, v_new, cache_kt, cache_vt, k_tail, v_tail, *, t_len):
    bsz, s_len, bw = q.shape
    n_seq, groups, fw, wb = cache_kt.shape
    n_pairs = bw // LANES
    assert s_len % ATTN_TILE == 0 and bw % LANES == 0 and bsz * n_pairs == n_seq * groups
    assert n_seq * t_len <= LANES
    r = np.arange(ATTN_TILE)[:, None]
    c = np.arange(s_len)[None, :]
    bias = jnp.asarray(_log_multiplicity(r - (c - (s_len - ATTN_TILE))))
    t = np.repeat(np.arange(t_len), SUBLANES)[:, None]
    bias_c = jnp.asarray(_log_multiplicity(wb + t - np.arange(wb)[None, :]))
    tn = np.arange(SUBLANES)[None, :]
    bias_n = jnp.asarray(np.where(tn < t_len, _log_multiplicity(t - tn), NEG_INF).astype(np.float32))

    tok = pl.BlockSpec((1, s_len, LANES), lambda b, h: (b, 0, h))
    feat = pl.BlockSpec((1, LANES, s_len), lambda b, h: (b, h, 0))
    piece = lambda b, h: ((b * n_pairs + h) // groups, (b * n_pairs + h) % groups, 0, 0)
    small = pl.BlockSpec((1, 1, SUBLANES, fw), piece)
    big = pl.BlockSpec((1, 1, fw, wb), piece)
    tail = pl.BlockSpec((1, fw, LANES), lambda b, h: ((b * n_pairs + h) % groups, 0, 0))
    return pl.pallas_call(
        functools.partial(_attn_kernel, t_len=t_len, groups=groups),
        grid=(bsz, n_pairs),
        in_specs=[tok, feat, feat, _const_spec(bias.shape),
                  small, small, small, big, big, tail, tail,
                  _const_spec(bias_c.shape), _const_spec(bias_n.shape)],
        out_specs=[tok, small, big, big],
        out_shape=[jax.ShapeDtypeStruct((bsz, s_len, bw), BF16),
                   jax.ShapeDtypeStruct(q_s.shape, F32),
                   jax.ShapeDtypeStruct(cache_kt.shape, cache_kt.dtype),
                   jax.ShapeDtypeStruct(cache_vt.shape, cache_vt.dtype)],
        scratch_shapes=[pltpu.VMEM((LANES, s_len), BF16), pltpu.VMEM((2, LANES, s_len), BF16)],
        compiler_params=_params(("parallel", "parallel")),
        name="attn",
    )(q, k_t, v_t, bias, q_s, k_new, v_new, cache_kt, cache_vt, k_tail, v_tail, bias_c, bias_n)


N_POST_WEIGHTS = 11


def _interleave(main, side, ratio):
    for _ in main:
        for _ in range(ratio):
            next(side, None)
    for _ in side:
        pass


def _post_kernel(*refs, stride, ff_piece, gla_chunk):
    x_ref, oa_ref, ob_ref, p_ref, cin_ref = refs[:5]
    (wo_ref, gf_ref, wg_ref, wu_ref, cw_ref, cb_ref, wd_ref, gp_ref, wpg_ref, wpp_ref,
     gl_ref) = refs[5:5 + N_POST_WEIGHTS]
    rest = refs[5 + N_POST_WEIGHTS:]
    if gla_chunk:
        (qa_ref, fa_ref, va_ref, ga_ref, s0_ref, on_ref, y_ref, cout_ref, sa_ref,
         prev_ref, act_ref, oa_scr, b_scr) = rest
    else:
        y_ref, cout_ref, prev_ref, act_ref = rest
    ts = x_ref.shape[1]
    d_ff = wg_ref.shape[1]
    aw = oa_ref.shape[2]
    n_prev = prev_ref.shape[0]
    b_idx, j = pl.program_id(0), pl.program_id(1)

    @pl.when(j == 0)
    def _():
        prev_ref[...] = cin_ref[0]

    if gla_chunk:
        n_heads = oa_scr.shape[1]

        @pl.when((b_idx == 0) & (j == 0))
        def _():
            oa_scr[0] = jnp.zeros(oa_scr.shape[1:], oa_scr.dtype)

        tile_rows = pl.ds(pl.multiple_of(j * ts, ts), ts)
        oa_here = jnp.concatenate([oa_scr[b_idx % 2, h, tile_rows, :] for h in range(n_heads)], axis=1)
        oa = jnp.where(b_idx == 0, oa_ref[0], oa_here)
    else:
        oa = oa_ref[0]

    def post_steps():
        mix = jnp.dot(oa, wo_ref[0:aw, :], preferred_element_type=F32)
        yield
        mix = mix + jnp.dot(ob_ref[0], wo_ref[aw:, :], preferred_element_type=F32)
        yield
        h = x_ref[0] + mix
        hn = _rmsnorm(h, gf_ref[...]).astype(BF16)

        sub = lax.broadcasted_iota(jnp.int32, (SUBLANES, 1), 0)
        for c0 in range(0, d_ff, ff_piece):
            cs = slice(c0, c0 + ff_piece)
            u = jnp.dot(hn, wg_ref[:, cs], preferred_element_type=F32)
            yield
            up = jnp.dot(hn, wu_ref[:, cs], preferred_element_type=F32)
            yield
            prev = prev_ref[:, cs]
            if stride % SUBLANES == 0:
                u1 = jnp.concatenate([prev[stride:], u[:ts - stride]], axis=0)
                u2 = jnp.concatenate([prev, u[:ts - 2 * stride]], axis=0)
            else:
                assert stride == 1 and n_prev == SUBLANES
                r1, r2 = pltpu.roll(u, 1, 0), pltpu.roll(u, 2, 0)
                head1 = jnp.where(sub < 1, pltpu.roll(prev, 1, 0), r1[:SUBLANES])
                head2 = jnp.where(sub < 2, pltpu.roll(prev, 2, 0), r2[:SUBLANES])
                u1 = jnp.concatenate([head1, r1[SUBLANES:]], axis=0)
                u2 = jnp.concatenate([head2, r2[SUBLANES:]], axis=0)
            prev_ref[:, cs] = u[ts - n_prev:]
            conv = cb_ref[:, cs] + cw_ref[0:1, cs] * u2 + cw_ref[1:2, cs] * u1 + cw_ref[2:3, cs] * u
            act_ref[:, cs] = (_silu(conv) * up).astype(BF16)
        cout_ref[0] = prev_ref[...]

        h = h + jnp.dot(act_ref[...], wd_ref[...], preferred_element_type=F32)
        yield
        gate = jnp.dot(_rmsnorm(h, gp_ref[...]).astype(BF16), wpg_ref[...], preferred_element_type=F32)
        yield
        emb = jnp.dot(p_ref[0].astype(BF16), wpp_ref[...], preferred_element_type=F32)
        yield
        h = h + _sigmoid(gate) * emb
        y_ref[0] = _rmsnorm(h, gl_ref[...])

    def gla_steps():
        halves, level = _gla_levels(gla_chunk)
        onorm_g = on_ref[...]
        st = s0_ref[0, 0].T
        n_chunks = qa_ref.shape[1] // gla_chunk
        rows = lambda c: slice(c * gla_chunk, (c + 1) * gla_chunk)

        def scores_of(c, out):
            return _gla_scores_steps(qa_ref[0, rows(c), :], fa_ref[0, rows(c), :], b_scr.at[c], halves, level, out)

        scores = []
        yield from scores_of(0, scores)
        for c in range(n_chunks):
            ahead = []
            if c + 1 < n_chunks:
                yield from scores_of(c + 1, ahead)
            out = []
            yield from _gla_output_steps(scores, va_ref[0, rows(c), :], ga_ref[0, rows(c), :], st,
                                         b_scr.at[c], onorm_g, out)
            o, st = out
            oa_scr[(b_idx + 1) % 2, j, rows(c), :] = o.astype(oa_scr.dtype)
            scores = ahead
        sa_ref[0, 0] = st.T

    if gla_chunk:
        n_main = 5 + 2 * (d_ff // ff_piece)
        n_side = (qa_ref.shape[1] // gla_chunk) * (gla_chunk.bit_length() + 3)
        _interleave(post_steps(), gla_steps(), -(-n_side // n_main))
    else:
        for _ in post_steps():
            pass


def _post(x, oa, ob, p, conv_in, w_o, g_ffn, w_gate, w_up, conv_w, conv_b, w_down, g_ple, w_pg, w_pp,
          g_fin, *, ts, stride, gla=None):
    bsz, t_len, d = x.shape
    d_ff = w_gate.shape[1]
    n_prev = conv_in.shape[1]
    nt = t_len // ts
    ff_piece = MXU_WIDTH
    assert d_ff % ff_piece == 0 and t_len % ts == 0 and ts >= 2 * stride
    tok = lambda b, j: (b, j, 0)
    seq = lambda b, j: (b, 0, 0)
    row = lambda a: a.reshape(1, -1)
    weights = [w_o, row(g_ffn), w_gate, w_up, conv_w, row(conv_b), w_down, row(g_ple), w_pg, w_pp, row(g_fin)]
    assert len(weights) == N_POST_WEIGHTS
    in_specs = [pl.BlockSpec((1, ts, d), tok),
                pl.BlockSpec((1, ts, oa.shape[2]), tok),
                pl.BlockSpec((1, ts, ob.shape[2]), tok),
                pl.BlockSpec((1, ts, p.shape[2]), tok),
                pl.BlockSpec((1, n_prev, d_ff), seq)] + [_const_spec(w.shape) for w in weights]
    out_specs = [pl.BlockSpec((1, ts, d), tok), pl.BlockSpec((1, n_prev, d_ff), seq)]
    out_shape = [jax.ShapeDtypeStruct((bsz, t_len, d), F32), jax.ShapeDtypeStruct((bsz, n_prev, d_ff), F32)]
    scratch = [pltpu.VMEM((n_prev, d_ff), F32), pltpu.VMEM((ts, d_ff), BF16)]
    operands = [x, oa, ob, p, conv_in, *weights]
    gla_chunk = None
    if gla is not None:
        a_part, s0, onorm_g, gla_chunk = gla
        nh, hd = s0.shape[1], HGRN_HEAD_DIM
        assert nt == nh and t_len % gla_chunk == 0 and oa.shape[0] == 1
        nxt = lambda b: jnp.minimum(b + 1, bsz - 1)
        in_specs[1] = pl.BlockSpec((1, ts, oa.shape[2]), lambda b, j: (0, jnp.where(b == 0, j, 0), 0))
        col = lambda part: pl.BlockSpec((1, None, t_len, hd), lambda b, j, part=part: (nxt(b), part * nh + j, 0, 0))
        state = pl.BlockSpec((1, 1, hd, hd), lambda b, j: (nxt(b), j, 0, 0))
        in_specs += [col(0), col(1), col(2), col(3), state, pl.BlockSpec((1, hd), lambda b, j: (0, j))]
        out_specs.append(pl.BlockSpec((1, 1, hd, hd), lambda b, j: (b, j, 0, 0)))
        out_shape.append(jax.ShapeDtypeStruct(s0.shape, F32))
        scratch += [pltpu.VMEM((2, nh, t_len, hd), BF16), pltpu.VMEM((t_len // gla_chunk, gla_chunk, hd), F32)]
        operands += [a_part, a_part, a_part, a_part, s0, onorm_g.reshape(1, -1)]
    return pl.pallas_call(
        functools.partial(_post_kernel, stride=stride, ff_piece=ff_piece, gla_chunk=gla_chunk),
        grid=(bsz, nt),
        in_specs=in_specs,
        out_specs=out_specs,
        out_shape=out_shape,
        scratch_shapes=scratch,
        compiler_params=_params(("arbitrary", "arbitrary")),
        name="post",
    )(*operands)


PROMPT_TILE = 512
PROJ_TILE = 1024
GLA_CHUNK = 128
GLA_GROUP = 8
SAMPLE_SEQS = 16
SAMPLE_HEAD_GROUPS = 2


def _feature_major(cache):
    bsz, wb, nh, hd = cache.shape
    return cache.transpose(0, 2, 3, 1).reshape(bsz, nh * hd, wb)


def _window_major(cache_t, nh):
    bsz, bw, wb = cache_t.shape
    return cache_t.reshape(bsz, nh, bw // nh, wb).transpose(0, 3, 1, 2)


def kernel(x_prompt, x_sample, state_hgrn, cache_win_k, cache_win_v, state_ffn_conv, p_prompt, p_sample,
           norm_attn_g, w_in, hgrn_lb_logits, hgrn_onorm_g, w_o, norm_ffn_g, w_gate, w_up, conv_w, conv_b,
           w_down, norm_ple_g, w_ple_gate, w_ple_proj, norm_final_g):
    depth = w_in.shape[0]
    assert depth == 1, "single-layer step"
    i = 0
    bsz, s_len, d = x_prompt.shape
    dbs, t_dec, _ = x_sample.shape
    nh_a, hd = state_hgrn.shape[2], state_hgrn.shape[3]
    wb, nh_b = cache_win_k.shape[2], cache_win_k.shape[3]
    bw = nh_b * HD_B
    aw = nh_a * hd
    d_ff = w_gate.shape[2]
    assert wb == min(WIN_MAX, PAST_LEN) and s_len <= WIN_MAX and t_dec <= SUBLANES
    n_conv = CONV_W - 1

    w_aq16 = w_in[i][:, :4 * aw + bw].astype(BF16)
    w_kv16_t = w_in[i][:, 4 * aw + bw:].T.astype(BF16)
    w_gate16, w_up16, w_down16 = w_gate[i].astype(BF16), w_up[i].astype(BF16), w_down[i].astype(BF16)
    post_w = (w_o[i].astype(BF16), norm_ffn_g[i], w_gate16, w_up16, conv_w[i], conv_b[i], w_down16,
              norm_ple_g[i], w_ple_gate[i].astype(BF16), w_ple_proj[i].astype(BF16), norm_final_g)

    a_p, qb_p, kt_p, vt_p = _proj(x_prompt, norm_attn_g[i], w_aq16, w_kv16_t, hgrn_lb_logits,
                                  _rotary_tables(jnp.arange(s_len, dtype=F32)), layer=i, ts=min(PROJ_TILE, s_len))
    n_tok = dbs * t_dec
    pos_s = jnp.tile(float(PAST_LEN) + jnp.arange(t_dec, dtype=F32), dbs)
    a_s, qb_s, kt_s, vt_s = _proj(x_sample.reshape(1, n_tok, d), norm_attn_g[i], w_aq16, w_kv16_t,
                                  hgrn_lb_logits, _rotary_tables(pos_s), layer=i, ts=n_tok)

    def per_seq(a, fill=0.0):
        a = a.reshape(dbs, t_dec, a.shape[-1]).astype(F32)
        return jnp.pad(a, ((0, 0), (0, SUBLANES - t_dec), (0, 0)), constant_values=fill)

    def tile_rows(a):
        return a[:, :t_dec].transpose(1, 0, 2).reshape(1, n_tok, a.shape[-1])

    s0_p = jnp.zeros((bsz, nh_a, hd, hd), F32)
    gla_chunk = min(GLA_CHUNK, s_len)
    oa_first, s_first = _gla(a_p, s0_p, hgrn_onorm_g[i], chunk=gla_chunk, group=GLA_GROUP, n_seq=1, n_batches=1)
    a_s = a_s[0].reshape(4, nh_a, dbs, t_dec, hd).transpose(2, 0, 1, 3, 4)
    pad_rows = lambda a, fill: jnp.pad(a, ((0, 0),) * 3 + ((0, SUBLANES - t_dec), (0, 0)), constant_values=fill)
    a_s = jnp.concatenate([pad_rows(a_s[:, :1], 0.0), pad_rows(a_s[:, 1:2], 1.0), pad_rows(a_s[:, 2:], 0.0)],
                          axis=1).reshape(dbs, 4 * nh_a, SUBLANES, hd)
    oa_s, s_s = _gla(a_s, state_hgrn[i].astype(F32), hgrn_onorm_g[i], chunk=SUBLANES, group=1,
                     n_seq=SAMPLE_SEQS)

    grp = SAMPLE_HEAD_GROUPS
    fw = bw // grp

    def head_groups(a):
        return a.reshape(dbs, SUBLANES, grp, fw).transpose(0, 2, 1, 3)

    def tail(a_t):
        return jnp.pad(a_t.reshape(grp, fw, n_tok), ((0, 0), (0, 0), (0, LANES - n_tok)))

    ob_p, ob_s, kt_new, vt_new = _attention(
        qb_p, kt_p, vt_p,
        head_groups(per_seq(qb_s[0])), head_groups(per_seq(kt_s[0].T)), head_groups(per_seq(vt_s[0].T)),
        _feature_major(cache_win_k[i]).reshape(dbs, grp, fw, wb),
        _feature_major(cache_win_v[i]).reshape(dbs, grp, fw, wb),
        tail(kt_s[0]), tail(vt_s[0]), t_len=t_dec)
    ob_s = ob_s.transpose(0, 2, 1, 3).reshape(dbs, SUBLANES, bw)
    kt_new, vt_new = kt_new.reshape(dbs, bw, wb), vt_new.reshape(dbs, bw, wb)

    y_p, conv_p, s_rest = _post(x_prompt, oa_first, ob_p, p_prompt[i], jnp.zeros((bsz, SUBLANES, d_ff), F32),
                                *post_w, ts=PROMPT_TILE, stride=1,
                                gla=(a_p, s0_p, hgrn_onorm_g[i], gla_chunk))
    s_p = jnp.concatenate([s_first, s_rest[:bsz - 1]], axis=0)
    conv_in_s = state_ffn_conv[i].transpose(1, 0, 2).reshape(1, n_conv * dbs, d_ff)
    p_s = p_sample[i].transpose(1, 0, 2).reshape(1, n_tok, -1)
    x_s = x_sample.transpose(1, 0, 2).reshape(1, n_tok, d)
    y_s, conv_s = _post(x_s, tile_rows(oa_s), tile_rows(ob_s).astype(BF16), p_s, conv_in_s, *post_w,
                        ts=n_tok, stride=dbs)
    keep = min(WIN_MAX, s_len)

    return (y_p,
            y_s.reshape(t_dec, dbs, d).transpose(1, 0, 2),
            s_p[None],
            _window_major(kt_p, nh_b)[None, :, s_len - keep:],
            _window_major(vt_p, nh_b)[None, :, s_len - keep:],
            conv_p[None, :, SUBLANES - n_conv:],
            s_s[None],
            _window_major(kt_new, nh_b)[None],
            _window_major(vt_new, nh_b)[None],
            conv_s.reshape(n_conv, dbs, d_ff).transpose(1, 0, 2)[None])
```
